```python
import math
import jax, jax.numpy as jnp
from jax import lax
import numpy as np

D_MODEL = 2048
BATCH = 16
SEQ = 2048
DEPTH = 2
DEC_BATCH = 16
DEC_SEQ = 16
PAST_LEN = 1024

CHUNK = 64
Q_BLOCK = 128
N_LAYERS_A = (DEPTH + 1) // 2
N_LAYERS_B = DEPTH // 2

A_HEAD_DIM = 64
A_HEADS = D_MODEL // (2 * A_HEAD_DIM)
A_QK = 2 * A_HEADS * A_HEAD_DIM
A_V = A_HEADS * 2 * A_HEAD_DIM
A_IN = 2 * A_QK + 2 * A_V
A_ROT_DIM = A_HEAD_DIM // 4
ROPE_THETA = 500000.0

B_HEADS = 8
B_QK = D_MODEL
B_V = 2 * D_MODEL
B_DK = B_QK // B_HEADS
B_DV = B_V // B_HEADS
B_IN = 2 * B_QK + 2 * B_V
XPOS_BASE = 10000.0

NORM_EPS = 1e-6
SUBLN_EPS = 1e-5

kernel_name = 'chunk_causal_diffattn_retention_hybrid_step'


def rmsnorm(x, g, eps):
    xf = x.astype(jnp.float32)
    y = xf * lax.rsqrt(jnp.mean(xf * xf, axis=-1, keepdims=True) + eps)
    return (y * g.astype(jnp.float32)).astype(x.dtype)


def head_rmsnorm(x, eps):
    xf = x.astype(jnp.float32)
    y = xf * lax.rsqrt(jnp.mean(xf * xf, axis=-1, keepdims=True) + eps)
    return y.astype(x.dtype)


def rotate(x, angles):
    half = angles.shape[-1]
    cos = jnp.cos(angles)[None, :, None, :].astype(x.dtype)
    sin = jnp.sin(angles)[None, :, None, :].astype(x.dtype)
    x1 = x[..., :half]
    x2 = x[..., half:2 * half]
    rest = x[..., 2 * half:]
    return jnp.concatenate([x1 * cos - x2 * sin, x1 * sin + x2 * cos, rest], axis=-1)


def partial_rope_angles(pos):
    inv = ROPE_THETA ** (-jnp.arange(0, A_ROT_DIM, 2, dtype=jnp.float32) / A_ROT_DIM)
    return pos[:, None] * inv[None, :]


def xpos_angles(pos):
    inv = 1.0 / (XPOS_BASE ** jnp.linspace(0.0, 1.0, B_DK // 2, dtype=jnp.float32))
    return pos[:, None] * inv[None, :]


def diff_attend(q, k, v, lam, mask):
    b, lq = q.shape[0], q.shape[1]
    lk = k.shape[1]
    s = jnp.einsum('bqhd,bkhd->bhqk', q, k).astype(jnp.float32)
    if mask is not None:
        s = jnp.where(mask, s, -jnp.inf)
    p = jax.nn.softmax(s, axis=-1).reshape(b, A_HEADS, 2, lq, lk)
    a = p[:, :, 0] - lam * p[:, :, 1]
    return jnp.einsum('bhqk,bkhe->bqhe', a.astype(v.dtype), v)


def diff_layer(xp, xs, ck, cv, norm_g, w_in, lq1, lk1, lq2, lk2, subln_g, w_out, layer_idx):
    lambda_init = 0.8 - 0.6 * math.exp(-0.3 * layer_idx)
    lam = (jnp.exp(jnp.sum(lq1 * lk1).astype(jnp.float32))
           - jnp.exp(jnp.sum(lq2 * lk2).astype(jnp.float32)) + lambda_init)
    scale = A_HEAD_DIM ** -0.5

    def project(x, pos):
        b, l = x.shape[0], x.shape[1]
        z = rmsnorm(x, norm_g, NORM_EPS) @ w_in
        q = z[..., :A_QK].reshape(b, l, 2 * A_HEADS, A_HEAD_DIM)
        k = z[..., A_QK:2 * A_QK].reshape(b, l, 2 * A_HEADS, A_HEAD_DIM)
        v = z[..., 2 * A_QK:2 * A_QK + A_V].reshape(b, l, A_HEADS, 2 * A_HEAD_DIM)
        g = z[..., 2 * A_QK + A_V:]
        ang = partial_rope_angles(pos)
        return rotate(q, ang) * scale, rotate(k, ang), v, g

    def finish(x, o, g):
        b, l = x.shape[0], x.shape[1]
        o = rmsnorm(o, subln_g, SUBLN_EPS) * (1.0 - lambda_init)
        return x + (o.reshape(b, l, A_V) * jax.nn.silu(g)) @ w_out

    s_len = xp.shape[1]
    qp, kp, vp, gp = project(xp, jnp.arange(s_len, dtype=jnp.float32))
    outs = []
    for blk in range(s_len // Q_BLOCK):
        q0 = blk * Q_BLOCK
        kend = q0 + Q_BLOCK
        q_chunk = (q0 + jnp.arange(Q_BLOCK)) // CHUNK
        k_chunk = jnp.arange(kend) // CHUNK
        mask = (k_chunk[None, :] <= q_chunk[:, None])[None, None]
        outs.append(diff_attend(qp[:, q0:kend], kp[:, :kend], vp[:, :kend], lam, mask))
    yp = finish(xp, jnp.concatenate(outs, axis=1), gp)

    past = ck.shape[1]
    qs, ks, vs, gs = project(xs, past + jnp.arange(xs.shape[1], dtype=jnp.float32))
    k_all = jnp.concatenate([ck, ks], axis=1)
    v_all = jnp.concatenate([cv, vs], axis=1)
    ys = finish(xs, diff_attend(qs, k_all, v_all, lam, None), gs)
    return yp, ys, kp, vp, ks, vs


def retention_chunk(s_state, q, k, v, log_gamma):
    l = q.shape[1]
    idx = jnp.arange(l, dtype=jnp.float32)
    lg = log_gamma[:, None]
    intra = jnp.exp(lg[:, :, None] * jnp.abs(idx[:, None] - idx[None, :])).astype(q.dtype)
    scores = jnp.einsum('bqhd,bkhd->bhqk', q, k) * intra[None]
    y = jnp.einsum('bhqk,bkhe->bqhe', scores, v)
    q_decay = jnp.exp(lg * (idx + 1.0)).T.astype(q.dtype)
    y = y + jnp.einsum('bqhd,bhde->bqhe', q * q_decay[None, :, :, None], s_state)
    k_decay = jnp.exp(lg * (l - 1.0 - idx)).T.astype(q.dtype)
    chunk_decay = jnp.exp(log_gamma * l).astype(q.dtype)[None, :, None, None]
    s_new = chunk_decay * s_state + jnp.einsum('bkhd,bkhe->bhde', k * k_decay[None, :, :, None], v)
    return s_new, y


def retention_layer(xp, xs, st, norm_g, w_in, w_out):
    log_gamma = jnp.log1p(-jnp.exp2(-5.0 - jnp.arange(B_HEADS, dtype=jnp.float32)))

    def project(x, pos):
        b, l = x.shape[0], x.shape[1]
        z = rmsnorm(x, norm_g, NORM_EPS) @ w_in
        q = z[..., :B_QK].reshape(b, l, B_HEADS, B_DK)
        k = z[..., B_QK:2 * B_QK].reshape(b, l, B_HEADS, B_DK)
        v = z[..., 2 * B_QK:2 * B_QK + B_V].reshape(b, l, B_HEADS, B_DV)
        g = z[..., 2 * B_QK + B_V:]
        ang = xpos_angles(pos)
        return rotate(q, ang), rotate(k, ang) * (B_DK ** -0.5), v, g

    def finish(x, o, g):
        b, l = x.shape[0], x.shape[1]
        o = head_rmsnorm(o, NORM_EPS).reshape(b, l, B_V)
        return x + (o * jax.nn.silu(g)) @ w_out

    b, s_len = xp.shape[0], xp.shape[1]
    n_chunks = s_len // CHUNK
    qp, kp, vp, gp = project(xp, jnp.arange(s_len, dtype=jnp.float32))
    to_chunks = lambda a: a.reshape(b, n_chunks, CHUNK, *a.shape[2:]).swapaxes(0, 1)
    s0 = jnp.zeros((b, B_HEADS, B_DK, B_DV), qp.dtype)
    s_final, yc = lax.scan(lambda s, xs_: retention_chunk(s, xs_[0], xs_[1], xs_[2], log_gamma),
                           s0, (to_chunks(qp), to_chunks(kp), to_chunks(vp)))
    op = yc.swapaxes(0, 1).reshape(b, s_len, B_HEADS, B_DV)
    yp = finish(xp, op, gp)

    qs, ks, vs, gs = project(xs, PAST_LEN + jnp.arange(xs.shape[1], dtype=jnp.float32))
    s_new, os_ = retention_chunk(st.astype(qs.dtype), qs, ks, vs, log_gamma)
    ys = finish(xs, os_, gs)
    return yp, ys, s_final, s_new


def setup_inputs(seed: int = 0) -> dict:
    key = jax.random.key(seed)
    ks = jax.random.split(key, 20)
    nrm = jax.random.normal
    f32 = jnp.float32
    return {
        'x_prompt': nrm(ks[0], (BATCH, SEQ, D_MODEL), f32),
        'x_sample': nrm(ks[1], (DEC_BATCH, DEC_SEQ, D_MODEL), f32),
        'cache_k_a': nrm(ks[2], (N_LAYERS_A, DEC_BATCH, PAST_LEN, 2 * A_HEADS, A_HEAD_DIM), f32),
        'cache_v_a': nrm(ks[3], (N_LAYERS_A, DEC_BATCH, PAST_LEN, A_HEADS, 2 * A_HEAD_DIM), f32),
        'state_ret': 0.5 * nrm(ks[4], (N_LAYERS_B, DEC_BATCH, B_HEADS, B_DK, B_DV), f32),
        'norm_a': 1.0 + 0.02 * nrm(ks[5], (N_LAYERS_A, D_MODEL), f32),
        'w_in_a': nrm(ks[6], (N_LAYERS_A, D_MODEL, A_IN), f32) * D_MODEL ** -0.5,
        'lambda_q1': 0.1 * nrm(ks[7], (N_LAYERS_A, A_HEAD_DIM), f32),
        'lambda_k1': 0.1 * nrm(ks[8], (N_LAYERS_A, A_HEAD_DIM), f32),
        'lambda_q2': 0.1 * nrm(ks[9], (N_LAYERS_A, A_HEAD_DIM), f32),
        'lambda_k2': 0.1 * nrm(ks[10], (N_LAYERS_A, A_HEAD_DIM), f32),
        'subln_a': 1.0 + 0.02 * nrm(ks[11], (N_LAYERS_A, 2 * A_HEAD_DIM), f32),
        'w_out_a': nrm(ks[12], (N_LAYERS_A, A_V, D_MODEL), f32) * A_V ** -0.5,
        'norm_b': 1.0 + 0.02 * nrm(ks[13], (N_LAYERS_B, D_MODEL), f32),
        'w_in_b': nrm(ks[14], (N_LAYERS_B, D_MODEL, B_IN), f32) * D_MODEL ** -0.5,
        'w_out_b': nrm(ks[15], (N_LAYERS_B, B_V, D_MODEL), f32) * B_V ** -0.5,
        'norm_final': 1.0 + 0.02 * nrm(ks[16], (D_MODEL,), f32),
    }


def reference(x_prompt, x_sample, cache_k_a, cache_v_a, state_ret, norm_a, w_in_a,
              lambda_q1, lambda_k1, lambda_q2, lambda_k2, subln_a, w_out_a,
              norm_b, w_in_b, w_out_b, norm_final):
    xp, xs = x_prompt, x_sample
    k_p, v_p, k_s, v_s, sr_p, sr_s = [], [], [], [], [], []
    for i in range(DEPTH):
        j = i // 2
        if i % 2 == 0:
            xp, xs, kp, vp, ks_, vs_ = diff_layer(
                xp, xs, cache_k_a[j], cache_v_a[j], norm_a[j], w_in_a[j],
                lambda_q1[j], lambda_k1[j], lambda_q2[j], lambda_k2[j], subln_a[j], w_out_a[j], i)
            k_p.append(kp)
            v_p.append(vp)
            k_s.append(ks_)
            v_s.append(vs_)
        else:
            xp, xs, sp, ss = retention_layer(xp, xs, state_ret[j], norm_b[j], w_in_b[j], w_out_b[j])
            sr_p.append(sp)
            sr_s.append(ss)
    y_prompt = rmsnorm(xp, norm_final, NORM_EPS)
    y_sample = rmsnorm(xs, norm_final, NORM_EPS)
    new_k_a_prompt = jnp.stack(k_p)
    new_v_a_prompt = jnp.stack(v_p)
    new_state_ret_prompt = jnp.stack(sr_p)
    new_k_a_sample = jnp.stack(k_s)
    new_v_a_sample = jnp.stack(v_s)
    new_state_ret_sample = jnp.stack(sr_s)
    return (y_prompt, y_sample, new_k_a_prompt, new_v_a_prompt, new_state_ret_prompt,
            new_k_a_sample, new_v_a_sample, new_state_ret_sample)
```

```python
import functools
import math

import jax
import jax.numpy as jnp
from jax import lax
from jax.experimental import pallas as pl
from jax.experimental.pallas import tpu as pltpu

F32 = jnp.float32
BF16 = jnp.bfloat16

CHUNK = 64
A_HEAD_DIM = 64
A_ROT_DIM = A_HEAD_DIM // 4
A_ROT_HALF = A_ROT_DIM // 2
ROPE_THETA = 500000.0
B_HEADS = 8
XPOS_BASE = 10000.0
NORM_EPS = 1e-6
SUBLN_EPS = 1e-5

LANES = 128
VMEM_LIMIT_BYTES = 56 * 1024 * 1024


def _compiler_params(n_axes):
    return pltpu.CompilerParams(
        dimension_semantics=("arbitrary",) * n_axes,
        vmem_limit_bytes=VMEM_LIMIT_BYTES)


def _rms_scale(x, eps):
    return lax.rsqrt(jnp.mean(x * x, axis=-1, keepdims=True) + eps)


def _silu(g):
    return g * (1.0 / (1.0 + jnp.exp(-g)))


def _dot(a, b):
    return jnp.dot(a, b, preferred_element_type=F32)


def _dot_nt(a, b):
    return lax.dot_general(a, b, (((1,), (1,)), ((), ())), preferred_element_type=F32)


def _dot_tn(a, b):
    return lax.dot_general(a, b, (((0,), (0,)), ((), ())), preferred_element_type=F32)


def _normalize_rows(x_ref, nw_ref, xn_ref):
    @pl.when(pl.program_id(1) == 0)
    def _():
        x = x_ref[...]
        xn_ref[...] = (x * _rms_scale(x, NORM_EPS) * nw_ref[...]).astype(BF16)


def _inproj_a_kernel(x_ref, nw_ref, cos_ref, sa_ref, sb_ref, wq_ref, wk_ref, wv_ref, wg_ref,
                     q_ref, k_ref, v_ref, g_ref, xn_ref, *, q_scale):
    _normalize_rows(x_ref, nw_ref, xn_ref)
    xn = xn_ref[...]
    cos, sa, sb = cos_ref[...], sa_ref[...], sb_ref[...]

    def rope_store(z, out_ref, scale):
        for c in range(z.shape[1] // LANES):
            zc = z[:, c * LANES:(c + 1) * LANES]
            r = (zc * cos + pltpu.roll(zc, A_ROT_HALF, 1) * sa
                 + pltpu.roll(zc, LANES - A_ROT_HALF, 1) * sb)
            if scale != 1.0:
                r = r * scale
            out_ref[:, c * LANES:(c + 1) * LANES] = r.astype(out_ref.dtype)

    rope_store(_dot(xn, wq_ref[...]), q_ref, q_scale)
    rope_store(_dot(xn, wk_ref[...]), k_ref, 1.0)
    v_ref[...] = _dot(xn, wv_ref[...])
    g_ref[...] = _dot(xn, wg_ref[...])


def _inproj_b_kernel(x_ref, nw_ref, cos_ref, sin_ref, wq_ref, wk_ref, wv_ref, wg_ref,
                     q_ref, k_ref, v_ref, g_ref, xn_ref, *, k_scale):
    _normalize_rows(x_ref, nw_ref, xn_ref)
    xn = xn_ref[...]
    cos, sin = cos_ref[...], sin_ref[...]

    def rope_store(z, out_ref, scale):
        for hh in range(z.shape[1] // (2 * LANES)):
            c0 = hh * 2 * LANES
            lo = z[:, c0:c0 + LANES]
            hi = z[:, c0 + LANES:c0 + 2 * LANES]
            r_lo = lo * cos - hi * sin
            r_hi = lo * sin + hi * cos
            if scale != 1.0:
                r_lo, r_hi = r_lo * scale, r_hi * scale
            out_ref[:, c0:c0 + LANES] = r_lo.astype(out_ref.dtype)
            out_ref[:, c0 + LANES:c0 + 2 * LANES] = r_hi.astype(out_ref.dtype)

    rope_store(_dot(xn, wq_ref[...]), q_ref, 1.0)
    rope_store(_dot(xn, wk_ref[...]), k_ref, k_scale)
    v_ref[...] = _dot(xn, wv_ref[...]).astype(v_ref.dtype)
    g_ref[...] = _dot(xn, wg_ref[...])


def _inproj(name, kernel_fn, x, norm_w, tables, w_bf16, widths, out_dtypes, tm, tn_unit):
    t, d = x.shape
    nj = min(widths) // tn_unit
    tns = [w // nj for w in widths]
    offs = [sum(widths[:s]) // tns[s] for s in range(len(widths))]
    n_pos_blocks = tables[0].shape[0] // tm

    in_specs = [pl.BlockSpec((tm, d), lambda i, j: (i, 0)),
                pl.BlockSpec((1, d), lambda i, j: (0, 0))]
    in_specs += [pl.BlockSpec((tm, LANES), lambda i, j: (i % n_pos_blocks, 0)) for _ in tables]
    for s in range(len(widths)):
        in_specs.append(pl.BlockSpec((d, tns[s]), functools.partial(
            lambda i, j, off: (0, off + j), off=offs[s])))
    out_specs = [pl.BlockSpec((tm, tns[s]), lambda i, j: (i, j)) for s in range(len(widths))]
    out_shape = [jax.ShapeDtypeStruct((t, widths[s]), out_dtypes[s]) for s in range(len(widths))]
    return pl.pallas_call(
        kernel_fn,
        grid=(t // tm, nj),
        in_specs=in_specs,
        out_specs=out_specs,
        out_shape=out_shape,
        scratch_shapes=[pltpu.VMEM((tm, d), BF16)],
        compiler_params=_compiler_params(2),
        name=name,
    )(x, norm_w.reshape(1, d), *tables, *([w_bf16] * len(widths)))


def _lambda_value(lam_ref, lambda_init):
    lp = lam_ref[...]
    s1 = jnp.sum(lp[0:1] * lp[1:2], axis=-1, keepdims=True)
    s2 = jnp.sum(lp[2:3] * lp[3:4], axis=-1, keepdims=True)
    return jnp.exp(s1) - jnp.exp(s2) + lambda_init


def _split_subheads(q):
    lane = lax.broadcasted_iota(jnp.int32, q.shape, 1)
    qf = q.astype(F32)
    qa = jnp.where(lane < A_HEAD_DIM, qf, 0.0).astype(BF16)
    qb = jnp.where(lane >= A_HEAD_DIM, qf, 0.0).astype(BF16)
    return qa, qb


def _diff_finish(o1, o2, lam, sg, g, lambda_init):
    o = o1 - lam * o2
    on = o * _rms_scale(o, SUBLN_EPS) * sg
    on = on * (1.0 - lambda_init)
    return on * _silu(g)


def _attn_prompt_kernel(q_ref, k_ref, v_ref, g_ref, lam_ref, sg_ref, o_ref,
                        kb_ref, vb_ref, m_ref, l_ref, acc_ref, *, tq, lambda_init):
    seq = q_ref.shape[0]
    kb_ref[...] = k_ref[...].astype(BF16)
    vb_ref[...] = v_ref[...].astype(BF16)
    lam = _lambda_value(lam_ref, lambda_init)
    sg = sg_ref[...]
    row = lax.broadcasted_iota(jnp.int32, (tq, tq), 0)
    col = lax.broadcasted_iota(jnp.int32, (tq, tq), 1)
    diag_mask = (col // CHUNK) <= (row // CHUNK)

    def kv_step(qs, k0, masked):
        kt = kb_ref[pl.ds(k0, tq), :]
        vt = vb_ref[pl.ds(k0, tq), :]
        for i in range(2):
            s = _dot_nt(qs[i], kt)
            if masked:
                s = jnp.where(diag_mask, s, -jnp.inf)
            m_old = m_ref[i]
            m_new = jnp.maximum(m_old, jnp.max(s, axis=-1, keepdims=True))
            alpha = jnp.exp(m_old - m_new)
            p = jnp.exp(s - m_new)
            l_ref[i] = alpha * l_ref[i] + jnp.sum(p, axis=-1, keepdims=True)
            acc_ref[i] = alpha * acc_ref[i] + _dot(p.astype(BF16), vt)
            m_ref[i] = m_new

    def q_tile(qi, carry):
        q0 = pl.multiple_of(qi * tq, tq)
        qs = _split_subheads(q_ref[pl.ds(q0, tq), :])
        m_ref[...] = jnp.full(m_ref.shape, -jnp.inf, F32)
        l_ref[...] = jnp.zeros(l_ref.shape, F32)
        acc_ref[...] = jnp.zeros(acc_ref.shape, F32)

        def full_step(t, c):
            kv_step(qs, pl.multiple_of(t * tq, tq), False)
            return c

        lax.fori_loop(0, qi, full_step, 0)
        kv_step(qs, q0, True)
        o1 = acc_ref[0] * (1.0 / l_ref[0])
        o2 = acc_ref[1] * (1.0 / l_ref[1])
        out = _diff_finish(o1, o2, lam, sg, g_ref[pl.ds(q0, tq), :], lambda_init)
        o_ref[pl.ds(q0, tq), :] = out.astype(o_ref.dtype)
        return carry

    lax.fori_loop(0, seq // tq, q_tile, 0)


def _attn_prompt(q, k, v, g, lam_params, subln, batch, seq, lambda_init, tq):
    t, width = q.shape
    heads = width // LANES
    blk = lambda: pl.BlockSpec((seq, LANES), lambda b, h: (b, h))
    return pl.pallas_call(
        functools.partial(_attn_prompt_kernel, tq=tq, lambda_init=lambda_init),
        grid=(batch, heads),
        in_specs=[blk(), blk(), blk(), blk(),
                  pl.BlockSpec(lam_params.shape, lambda b, h: (0, 0)),
                  pl.BlockSpec((1, LANES), lambda b, h: (0, 0))],
        out_specs=blk(),
        out_shape=jax.ShapeDtypeStruct((t, width), BF16),
        scratch_shapes=[pltpu.VMEM((seq, LANES), BF16), pltpu.VMEM((seq, LANES), BF16),
                        pltpu.VMEM((2, tq, 1), F32), pltpu.VMEM((2, tq, 1), F32),
                        pltpu.VMEM((2, tq, LANES), F32)],
        compiler_params=_compiler_params(2),
        name="attn_prompt",
    )(q, k, v, g, lam_params, subln.reshape(1, LANES))


def _attn_sample_kernel(q_ref, kn_ref, vn_ref, ck_ref, cv_ref, g_ref, lam_ref, sg_ref, o_ref,
                        *, heads_per_step, lambda_init):
    lam = _lambda_value(lam_ref, lambda_init)
    sg = sg_ref[...]
    for hh in range(heads_per_step):
        cs = slice(hh * LANES, (hh + 1) * LANES)
        qs = _split_subheads(q_ref[:, cs])
        kc = ck_ref[:, cs].astype(BF16)
        vc = cv_ref[:, cs].astype(BF16)
        kn = kn_ref[:, cs].astype(BF16)
        vn = vn_ref[:, cs].astype(BF16)
        outs = []
        for i in range(2):
            sc = _dot_nt(qs[i], kc)
            sn = _dot_nt(qs[i], kn)
            m = jnp.maximum(jnp.max(sc, axis=-1, keepdims=True),
                            jnp.max(sn, axis=-1, keepdims=True))
            pc = jnp.exp(sc - m)
            pn = jnp.exp(sn - m)
            l = jnp.sum(pc, axis=-1, keepdims=True) + jnp.sum(pn, axis=-1, keepdims=True)
            acc = _dot(pc.astype(BF16), vc) + _dot(pn.astype(BF16), vn)
            outs.append(acc * (1.0 / l))
        out = _diff_finish(outs[0], outs[1], lam, sg, g_ref[:, cs], lambda_init)
        o_ref[:, cs] = out.astype(o_ref.dtype)


def _attn_sample(q, k, v, g, cache_k, cache_v, lam_params, subln, batch, lambda_init,
                 heads_per_step):
    t, width = q.shape
    frames = t // batch
    past = cache_k.shape[0] // batch
    cols = heads_per_step * LANES
    new = lambda: pl.BlockSpec((frames, cols), lambda b, h: (b, h))
    old = lambda: pl.BlockSpec((past, cols), lambda b, h: (b, h))
    return pl.pallas_call(
        functools.partial(_attn_sample_kernel, heads_per_step=heads_per_step,
                          lambda_init=lambda_init),
        grid=(batch, width // cols),
        in_specs=[new(), new(), new(), old(), old(), new(),
                  pl.BlockSpec(lam_params.shape, lambda b, h: (0, 0)),
                  pl.BlockSpec((1, LANES), lambda b, h: (0, 0))],
        out_specs=new(),
        out_shape=jax.ShapeDtypeStruct((t, width), BF16),
        compiler_params=_compiler_params(2),
        name="attn_sample",
    )(q, k, v, cache_k, cache_v, g, lam_params, subln.reshape(1, LANES))


def _outproj_kernel(a_ref, w_ref, x_ref, y_ref):
    y_ref[...] = x_ref[...] + _dot(a_ref[...], w_ref[...])


def _outproj_final_kernel(a_ref, w_ref, x_ref, nf_ref, y_ref):
    xo = x_ref[...] + _dot(a_ref[...], w_ref[...])
    y_ref[...] = xo * _rms_scale(xo, NORM_EPS) * nf_ref[...]


def _outproj(name, a, w_bf16, x, tm, final_norm_w=None):
    t, kdim = a.shape
    d = x.shape[1]
    in_specs = [pl.BlockSpec((tm, kdim), lambda i: (i, 0)),
                pl.BlockSpec((kdim, d), lambda i: (0, 0), pipeline_mode=pl.Buffered(1)),
                pl.BlockSpec((tm, d), lambda i: (i, 0))]
    args = [a, w_bf16, x]
    kernel_fn = _outproj_kernel
    if final_norm_w is not None:
        in_specs.append(pl.BlockSpec((1, d), lambda i: (0, 0)))
        args.append(final_norm_w.reshape(1, d))
        kernel_fn = _outproj_final_kernel
    return pl.pallas_call(
        kernel_fn,
        grid=(t // tm,),
        in_specs=in_specs,
        out_specs=pl.BlockSpec((tm, d), lambda i: (i, 0)),
        out_shape=jax.ShapeDtypeStruct((t, d), F32),
        compiler_params=_compiler_params(1),
        name=name,
    )(*args)


def _retention_kernel(*refs, rows, chunk, has_init):
    if has_init:
        lg_ref, q_ref, k_ref, v_ref, g_ref, s0_ref, o_ref, s_ref = refs
    else:
        lg_ref, q_ref, k_ref, v_ref, g_ref, o_ref, s_ref = refs
    n_blocks = q_ref.shape[0] // rows
    lg = lg_ref[pl.program_id(1)]
    row = lax.broadcasted_iota(jnp.int32, (rows, rows), 0)
    col = lax.broadcasted_iota(jnp.int32, (rows, rows), 1)
    dist = jnp.abs(row - col).astype(F32)
    decay = jnp.where((col // chunk) <= (row // chunk), jnp.exp(lg * dist), 0.0)
    idx = lax.broadcasted_iota(jnp.int32, (rows, 1), 0).astype(F32)
    q_decay = jnp.exp(lg * (idx + 1.0))
    k_decay = jnp.exp(lg * (rows - 1.0 - idx))
    block_decay = jnp.exp(jnp.full((1, 1), rows, F32) * lg)

    if has_init:
        s_ref[...] = s0_ref[...]
    else:
        s_ref[...] = jnp.zeros(s_ref.shape, F32)

    def block(r, carry):
        r0 = pl.multiple_of(r * rows, rows)
        q = q_ref[pl.ds(r0, rows), :]
        k = k_ref[pl.ds(r0, rows), :]
        v = v_ref[pl.ds(r0, rows), :]
        state = s_ref[...]
        scores = _dot_nt(q, k) * decay
        y = _dot(scores.astype(BF16), v) + q_decay * _dot(q, state.astype(BF16))
        kd = (k.astype(F32) * k_decay).astype(BF16)
        s_ref[...] = block_decay * state + _dot_tn(kd, v)
        yn = y * _rms_scale(y, NORM_EPS)
        o_ref[pl.ds(r0, rows), :] = (yn * _silu(g_ref[pl.ds(r0, rows), :])).astype(o_ref.dtype)
        return carry

    lax.fori_loop(0, n_blocks, block, 0)


def _retention(name, log_gamma, q, k, v, g, batch, rows, chunk, init_state=None):
    t = q.shape[0]
    seq = t // batch
    dk = q.shape[1] // B_HEADS
    dv = v.shape[1] // B_HEADS
    in_specs = [pl.BlockSpec(memory_space=pltpu.SMEM),
                pl.BlockSpec((seq, dk), lambda b, h: (b, h)),
                pl.BlockSpec((seq, dk), lambda b, h: (b, h)),
                pl.BlockSpec((seq, dv), lambda b, h: (b, h)),
                pl.BlockSpec((seq, dv), lambda b, h: (b, h))]
    args = [log_gamma, q, k, v, g]
    state_spec = lambda: pl.BlockSpec((None, None, dk, dv), lambda b, h: (b, h, 0, 0))
    if init_state is not None:
        in_specs.append(state_spec())
        args.append(init_state)
    return pl.pallas_call(
        functools.partial(_retention_kernel, rows=rows, chunk=chunk,
                          has_init=init_state is not None),
        grid=(batch, B_HEADS),
        in_specs=in_specs,
        out_specs=[pl.BlockSpec((seq, dv), lambda b, h: (b, h)), state_spec()],
        out_shape=[jax.ShapeDtypeStruct((t, v.shape[1]), BF16),
                   jax.ShapeDtypeStruct((batch, B_HEADS, dk, dv), F32)],
        compiler_params=_compiler_params(2),
        name=name,
    )(*args)


def _rope_tables_a(pos):
    inv = ROPE_THETA ** (-jnp.arange(0, A_ROT_DIM, 2, dtype=F32) / A_ROT_DIM)
    ang = pos[:, None] * inv[None, :]
    cos, sin = jnp.cos(ang), jnp.sin(ang)
    d = jnp.arange(LANES) % A_HEAD_DIM
    first = d < A_ROT_HALF
    second = (d >= A_ROT_HALF) & (d < A_ROT_DIM)
    idx = d % A_ROT_HALF
    cos_t = jnp.where((first | second)[None, :], cos[:, idx], 1.0)
    from_left = jnp.where(second[None, :], sin[:, idx], 0.0)
    from_right = jnp.where(first[None, :], -sin[:, idx], 0.0)
    return cos_t, from_left, from_right


def _xpos_tables_b(pos, dk):
    inv = 1.0 / (XPOS_BASE ** jnp.linspace(0.0, 1.0, dk // 2, dtype=F32))
    ang = pos[:, None] * inv[None, :]
    return jnp.cos(ang), jnp.sin(ang)


def _row_tile(t, cap):
    return cap if t % cap == 0 else t


def kernel(x_prompt, x_sample, cache_k_a, cache_v_a, state_ret, norm_a, w_in_a, lambda_q1,
           lambda_k1, lambda_q2, lambda_k2, subln_a, w_out_a, norm_b, w_in_b, w_out_b,
           norm_final):
    batch, seq, d = x_prompt.shape
    dec_batch, dec_seq, _ = x_sample.shape
    past = cache_k_a.shape[2]
    a_heads = d // (2 * A_HEAD_DIM)
    dk = d // B_HEADS
    dv = 2 * d // B_HEADS
    lambda_init = 0.8 - 0.6 * math.exp(-0.3 * 0)

    xp = x_prompt.reshape(batch * seq, d)
    xs = x_sample.reshape(dec_batch * dec_seq, d)

    w_in_a_b = w_in_a[0].astype(BF16)
    w_out_a_b = w_out_a[0].astype(BF16)
    w_in_b_b = w_in_b[0].astype(BF16)
    w_out_b_b = w_out_b[0].astype(BF16)
    lam_params = jnp.stack([lambda_q1[0], lambda_k1[0], lambda_q2[0], lambda_k2[0]])

    pos_p = jnp.arange(seq, dtype=F32)
    pos_s = jnp.tile(past + jnp.arange(dec_seq, dtype=F32), dec_batch)
    log_gamma = jnp.log1p(-jnp.exp2(-5.0 - jnp.arange(B_HEADS, dtype=F32)))

    tm_p = _row_tile(seq, 512)
    tm_s = dec_batch * dec_seq

    a_widths = [d, d, d, d]
    a_dtypes = [BF16, F32, F32, F32]
    a_kernel = functools.partial(_inproj_a_kernel, q_scale=A_HEAD_DIM ** -0.5)
    qp, kp, vp, gp = _inproj("inproj_a_prompt", a_kernel, xp, norm_a[0], _rope_tables_a(pos_p),
                             w_in_a_b, a_widths, a_dtypes, tm_p, 256)
    qs, ks, vs, gs = _inproj("inproj_a_sample", a_kernel, xs, norm_a[0], _rope_tables_a(pos_s),
                             w_in_a_b, a_widths, a_dtypes, tm_s, 256)
    op = _attn_prompt(qp, kp, vp, gp, lam_params, subln_a[0], batch, seq, lambda_init,
                      tq=min(256, seq))
    os_ = _attn_sample(qs, ks, vs, gs,
                       cache_k_a[0].reshape(dec_batch * past, d),
                       cache_v_a[0].reshape(dec_batch * past, d),
                       lam_params, subln_a[0], dec_batch, lambda_init, heads_per_step=4)
    xp1 = _outproj("outproj_a_prompt", op, w_out_a_b, xp, tm_p)
    xs1 = _outproj("outproj_a_sample", os_, w_out_a_b, xs, tm_s)

    b_widths = [d, d, 2 * d, 2 * d]
    b_dtypes = [BF16, BF16, BF16, F32]
    b_kernel = functools.partial(_inproj_b_kernel, k_scale=dk ** -0.5)
    qp2, kp2, vp2, gp2 = _inproj("inproj_b_prompt", b_kernel, xp1, norm_b[0],
                                 _xpos_tables_b(pos_p, dk), w_in_b_b, b_widths, b_dtypes, tm_p, 256)
    qs2, ks2, vs2, gs2 = _inproj("inproj_b_sample", b_kernel, xs1, norm_b[0],
                                 _xpos_tables_b(pos_s, dk), w_in_b_b, b_widths, b_dtypes, tm_s, 256)
    rp, state_p = _retention("retention_prompt", log_gamma, qp2, kp2, vp2, gp2, batch,
                             rows=min(256, seq), chunk=CHUNK)
    rs, state_s = _retention("retention_sample", log_gamma, qs2, ks2, vs2, gs2, dec_batch,
                             rows=dec_seq, chunk=dec_seq, init_state=state_ret[0])
    yp = _outproj("outproj_b_prompt", rp, w_out_b_b, xp1, tm_p, final_norm_w=norm_final)
    ys = _outproj("outproj_b_sample", rs, w_out_b_b, xs1, tm_s, final_norm_w=norm_final)

    return (yp.reshape(batch, seq, d),
            ys.reshape(dec_batch, dec_seq, d),
            kp.reshape(1, batch, seq, 2 * a_heads, A_HEAD_DIM),
            vp.reshape(1, batch, seq, a_heads, 2 * A_HEAD_DIM),
            state_p[None],
            ks.reshape(1, dec_batch, dec_seq, 2 * a_heads, A_HEAD_DIM),
            vs.reshape(1, dec_batch, dec_seq, a_heads, 2 * A_HEAD_DIM),
            state_s[None])
```

```python
import functools
import math

import jax
import jax.numpy as jnp
from jax import lax
from jax.experimental import pallas as pl
from jax.experimental.pallas import tpu as pltpu

F32 = jnp.float32
BF16 = jnp.bfloat16

CHUNK = 64
A_HEAD_DIM = 64
A_ROT_DIM = A_HEAD_DIM // 4
A_ROT_HALF = A_ROT_DIM // 2
ROPE_THETA = 500000.0
B_HEADS = 8
XPOS_BASE = 10000.0
NORM_EPS = 1e-6
SUBLN_EPS = 1e-5

LANES = 128
VMEM_LIMIT_BYTES = 56 * 1024 * 1024


def _compiler_params(n_axes):
    return pltpu.CompilerParams(
        dimension_semantics=("arbitrary",) * n_axes,
        vmem_limit_bytes=VMEM_LIMIT_BYTES)


def _rms_scale(x, eps):
    return lax.rsqrt(jnp.mean(x * x, axis=-1, keepdims=True) + eps)


def _silu(g):
    return g * (1.0 / (1.0 + jnp.exp(-g)))


def _dot(a, b):
    return jnp.dot(a, b, preferred_element_type=F32)


def _dot_nt(a, b):
    return lax.dot_general(a, b, (((1,), (1,)), ((), ())), preferred_element_type=F32)


def _dot_tn(a, b):
    return lax.dot_general(a, b, (((0,), (0,)), ((), ())), preferred_element_type=F32)


def _normalize_rows(x_ref, nw_ref, xn_ref):
    @pl.when(pl.program_id(1) == 0)
    def _():
        x = x_ref[...]
        xn_ref[...] = (x * _rms_scale(x, NORM_EPS) * nw_ref[...]).astype(BF16)


def _inproj_a_kernel(x_ref, nw_ref, cos_ref, sa_ref, sb_ref, wq_ref, wk_ref, wv_ref, wg_ref,
                     q_ref, k_ref, v_ref, g_ref, xn_ref, *, q_scale):
    _normalize_rows(x_ref, nw_ref, xn_ref)
    xn = xn_ref[...]
    cos, sa, sb = cos_ref[...], sa_ref[...], sb_ref[...]

    def rope_store(z, out_ref, scale):
        for c in range(z.shape[1] // LANES):
            zc = z[:, c * LANES:(c + 1) * LANES]
            r = (zc * cos + pltpu.roll(zc, A_ROT_HALF, 1) * sa
                 + pltpu.roll(zc, LANES - A_ROT_HALF, 1) * sb)
            if scale != 1.0:
                r = r * scale
            out_ref[:, c * LANES:(c + 1) * LANES] = r.astype(out_ref.dtype)

    rope_store(_dot(xn, wq_ref[...]), q_ref, q_scale)
    rope_store(_dot(xn, wk_ref[...]), k_ref, 1.0)
    v_ref[...] = _dot(xn, wv_ref[...])
    g_ref[...] = _dot(xn, wg_ref[...])


def _inproj_b_kernel(x_ref, nw_ref, cos_ref, sin_ref, wq_ref, wk_ref, wv_ref, wg_ref,
                     q_ref, k_ref, v_ref, g_ref, xn_ref, *, k_scale):
    _normalize_rows(x_ref, nw_ref, xn_ref)
    xn = xn_ref[...]
    cos, sin = cos_ref[...], sin_ref[...]

    def rope_store(z, out_ref, scale):
        for hh in range(z.shape[1] // (2 * LANES)):
            c0 = hh * 2 * LANES
            lo = z[:, c0:c0 + LANES]
            hi = z[:, c0 + LANES:c0 + 2 * LANES]
            r_lo = lo * cos - hi * sin
            r_hi = lo * sin + hi * cos
            if scale != 1.0:
                r_lo, r_hi = r_lo * scale, r_hi * scale
            out_ref[:, c0:c0 + LANES] = r_lo.astype(out_ref.dtype)
            out_ref[:, c0 + LANES:c0 + 2 * LANES] = r_hi.astype(out_ref.dtype)

    rope_store(_dot(xn, wq_ref[...]), q_ref, 1.0)
    rope_store(_dot(xn, wk_ref[...]), k_ref, k_scale)
    v_ref[...] = _dot(xn, wv_ref[...]).astype(v_ref.dtype)
    g_ref[...] = _dot(xn, wg_ref[...])


def _inproj(name, kernel_fn, x, norm_w, tables, w_bf16, widths, out_dtypes, tm, tn_unit):
    t, d = x.shape
    nj = min(widths) // tn_unit
    tns = [w // nj for w in widths]
    offs = [sum(widths[:s]) // tns[s] for s in range(len(widths))]
    n_pos_blocks = tables[0].shape[0] // tm

    in_specs = [pl.BlockSpec((tm, d), lambda i, j: (i, 0)),
                pl.BlockSpec((1, d), lambda i, j: (0, 0))]
    in_specs += [pl.BlockSpec((tm, LANES), lambda i, j: (i % n_pos_blocks, 0)) for _ in tables]
    for s in range(len(widths)):
        in_specs.append(pl.BlockSpec((d, tns[s]), functools.partial(
            lambda i, j, off: (0, off + j), off=offs[s])))
    out_specs = [pl.BlockSpec((tm, tns[s]), lambda i, j: (i, j)) for s in range(len(widths))]
    out_shape = [jax.ShapeDtypeStruct((t, widths[s]), out_dtypes[s]) for s in range(len(widths))]
    return pl.pallas_call(
        kernel_fn,
        grid=(t // tm, nj),
        in_specs=in_specs,
        out_specs=out_specs,
        out_shape=out_shape,
        scratch_shapes=[pltpu.VMEM((tm, d), BF16)],
        compiler_params=_compiler_params(2),
        name=name,
    )(x, norm_w.reshape(1, d), *tables, *([w_bf16] * len(widths)))


def _lambda_value(lam_ref, lambda_init):
    lp = lam_ref[...]
    s1 = jnp.sum(lp[0:1] * lp[1:2], axis=-1, keepdims=True)
    s2 = jnp.sum(lp[2:3] * lp[3:4], axis=-1, keepdims=True)
    return jnp.exp(s1) - jnp.exp(s2) + lambda_init


def _split_subheads(q):
    lane = lax.broadcasted_iota(jnp.int32, q.shape, 1)
    qf = q.astype(F32)
    return jnp.where(lane < A_HEAD_DIM, qf, 0.0), jnp.where(lane >= A_HEAD_DIM, qf, 0.0)


def _diff_finish(o1, o2, lam, sg, g, lambda_init):
    o = o1 - lam * o2
    on = o * _rms_scale(o, SUBLN_EPS) * sg
    on = on * (1.0 - lambda_init)
    return on * _silu(g)


def _attn_prompt_kernel(q_ref, k_ref, v_ref, g_ref, lam_ref, sg_ref, o_ref,
                        kb_ref, vt_ref, qt_ref, m_ref, l_ref, acc_ref, s_ref, p_ref, a_ref,
                        *, tq, lambda_init):
    seq = q_ref.shape[0]
    n_tiles = seq // tq
    kb_ref[...] = k_ref[...].astype(BF16)
    for c in range(n_tiles):
        vt_ref[c] = v_ref[c * tq:(c + 1) * tq, :].T.astype(BF16)
    lam = _lambda_value(lam_ref, lambda_init)
    sg = sg_ref[...]
    for c in range(n_tiles):
        qa, qb = _split_subheads(q_ref[c * tq:(c + 1) * tq, :])
        qt_ref[c, 0] = qa.T.astype(BF16)
        qt_ref[c, 1] = qb.T.astype(BF16)
    m_ref[...] = jnp.full(m_ref.shape, -jnp.inf, F32)
    l_ref[...] = jnp.zeros(l_ref.shape, F32)
    acc_ref[...] = jnp.zeros(acc_ref.shape, F32)
    key = lax.broadcasted_iota(jnp.int32, (tq, tq), 0)
    qry = lax.broadcasted_iota(jnp.int32, (tq, tq), 1)
    diag_mask = (key // CHUNK) <= (qry // CHUNK)

    items = [(qi, kt) for qi in range(n_tiles) for kt in range(qi)]
    items += [(qi, qi) for qi in range(n_tiles)]

    def score_stage(j):
        qi, kt = items[j]
        for i in range(2):
            s_ref[j % 2, i] = _dot(kb_ref[kt * tq:(kt + 1) * tq, :], qt_ref[qi, i])

    def softmax_stage(j):
        qi, kt = items[j]
        for i in range(2):
            s = s_ref[j % 2, i]
            if kt == qi:
                s = jnp.where(diag_mask, s, -jnp.inf)
            m_old = m_ref[qi, i]
            m_new = jnp.maximum(m_old, jnp.max(s, axis=0, keepdims=True))
            alpha = jnp.exp(m_old - m_new)
            p = jnp.exp(s - m_new)
            l_ref[qi, i] = alpha * l_ref[qi, i] + jnp.sum(p, axis=0, keepdims=True)
            m_ref[qi, i] = m_new
            a_ref[j % 2, i] = alpha
            p_ref[j % 2, i] = p.astype(BF16)

    def value_stage(j):
        qi, kt = items[j]
        for i in range(2):
            acc_ref[qi, i] = (a_ref[j % 2, i] * acc_ref[qi, i]
                              + _dot(vt_ref[kt], p_ref[j % 2, i]))

    for step in range(len(items) + 2):
        if 0 <= step - 2:
            value_stage(step - 2)
        if 0 <= step - 1 < len(items):
            softmax_stage(step - 1)
        if step < len(items):
            score_stage(step)

    for qi in range(n_tiles):
        o1 = (acc_ref[qi, 0] * (1.0 / l_ref[qi, 0])).T
        o2 = (acc_ref[qi, 1] * (1.0 / l_ref[qi, 1])).T
        out = _diff_finish(o1, o2, lam, sg, g_ref[qi * tq:(qi + 1) * tq, :], lambda_init)
        o_ref[qi * tq:(qi + 1) * tq, :] = out.astype(o_ref.dtype)


def _attn_prompt(q, k, v, g, lam_params, subln, batch, seq, lambda_init, tq):
    t, width = q.shape
    heads = width // LANES
    n_tiles = seq // tq
    blk = lambda: pl.BlockSpec((seq, LANES), lambda b, h: (b, h))
    return pl.pallas_call(
        functools.partial(_attn_prompt_kernel, tq=tq, lambda_init=lambda_init),
        grid=(batch, heads),
        in_specs=[blk(), blk(), blk(), blk(),
                  pl.BlockSpec(lam_params.shape, lambda b, h: (0, 0)),
                  pl.BlockSpec((1, LANES), lambda b, h: (0, 0))],
        out_specs=blk(),
        out_shape=jax.ShapeDtypeStruct((t, width), BF16),
        scratch_shapes=[pltpu.VMEM((seq, LANES), BF16),
                        pltpu.VMEM((n_tiles, LANES, tq), BF16),
                        pltpu.VMEM((n_tiles, 2, LANES, tq), BF16),
                        pltpu.VMEM((n_tiles, 2, 1, tq), F32),
                        pltpu.VMEM((n_tiles, 2, 1, tq), F32),
                        pltpu.VMEM((n_tiles, 2, LANES, tq), F32),
                        pltpu.VMEM((2, 2, tq, tq), F32),
                        pltpu.VMEM((2, 2, tq, tq), BF16),
                        pltpu.VMEM((2, 2, 1, tq), F32)],
        compiler_params=_compiler_params(2),
        name="attn_prompt",
    )(q, k, v, g, lam_params, subln.reshape(1, LANES))


def _attn_sample_kernel(q_ref, kn_ref, vn_ref, ck_ref, cv_ref, g_ref, lam_ref, sg_ref, o_ref,
                        *, heads_per_step, lambda_init):
    lam = _lambda_value(lam_ref, lambda_init)
    sg = sg_ref[...]
    for hh in range(heads_per_step):
        cs = slice(hh * LANES, (hh + 1) * LANES)
        qs = [x.astype(BF16) for x in _split_subheads(q_ref[:, cs])]
        kc = ck_ref[:, cs].astype(BF16)
        vc = cv_ref[:, cs].astype(BF16)
        kn = kn_ref[:, cs].astype(BF16)
        vn = vn_ref[:, cs].astype(BF16)
        outs = []
        for i in range(2):
            sc = _dot_nt(qs[i], kc)
            sn = _dot_nt(qs[i], kn)
            m = jnp.maximum(jnp.max(sc, axis=-1, keepdims=True),
                            jnp.max(sn, axis=-1, keepdims=True))
            pc = jnp.exp(sc - m)
            pn = jnp.exp(sn - m)
            l = jnp.sum(pc, axis=-1, keepdims=True) + jnp.sum(pn, axis=-1, keepdims=True)
            acc = _dot(pc.astype(BF16), vc) + _dot(pn.astype(BF16), vn)
            outs.append(acc * (1.0 / l))
        out = _diff_finish(outs[0], outs[1], lam, sg, g_ref[:, cs], lambda_init)
        o_ref[:, cs] = out.astype(o_ref.dtype)


def _attn_sample(q, k, v, g, cache_k, cache_v, lam_params, subln, batch, lambda_init,
                 heads_per_step):
    t, width = q.shape
    frames = t // batch
    past = cache_k.shape[0] // batch
    cols = heads_per_step * LANES
    new = lambda: pl.BlockSpec((frames, cols), lambda b, h: (b, h))
    old = lambda: pl.BlockSpec((past, cols), lambda b, h: (b, h))
    return pl.pallas_call(
        functools.partial(_attn_sample_kernel, heads_per_step=heads_per_step,
                          lambda_init=lambda_init),
        grid=(batch, width // cols),
        in_specs=[new(), new(), new(), old(), old(), new(),
                  pl.BlockSpec(lam_params.shape, lambda b, h: (0, 0)),
                  pl.BlockSpec((1, LANES), lambda b, h: (0, 0))],
        out_specs=new(),
        out_shape=jax.ShapeDtypeStruct((t, width), BF16),
        compiler_params=_compiler_params(2),
        name="attn_sample",
    )(q, k, v, cache_k, cache_v, g, lam_params, subln.reshape(1, LANES))


def _outproj_kernel(a_ref, w_ref, x_ref, y_ref):
    y_ref[...] = x_ref[...] + _dot(a_ref[...], w_ref[...])


def _outproj_final_kernel(a_ref, w_ref, x_ref, nf_ref, y_ref):
    xo = x_ref[...] + _dot(a_ref[...], w_ref[...])
    y_ref[...] = xo * _rms_scale(xo, NORM_EPS) * nf_ref[...]


def _outproj(name, a, w_bf16, x, tm, final_norm_w=None):
    t, kdim = a.shape
    d = x.shape[1]
    in_specs = [pl.BlockSpec((tm, kdim), lambda i: (i, 0)),
                pl.BlockSpec((kdim, d), lambda i: (0, 0), pipeline_mode=pl.Buffered(1)),
                pl.BlockSpec((tm, d), lambda i: (i, 0))]
    args = [a, w_bf16, x]
    kernel_fn = _outproj_kernel
    if final_norm_w is not None:
        in_specs.append(pl.BlockSpec((1, d), lambda i: (0, 0)))
        args.append(final_norm_w.reshape(1, d))
        kernel_fn = _outproj_final_kernel
    return pl.pallas_call(
        kernel_fn,
        grid=(t // tm,),
        in_specs=in_specs,
        out_specs=pl.BlockSpec((tm, d), lambda i: (i, 0)),
        out_shape=jax.ShapeDtypeStruct((t, d), F32),
        compiler_params=_compiler_params(1),
        name=name,
    )(*args)


def _retention_kernel(*refs, rows, chunk, has_init):
    if has_init:
        lg_ref, q_ref, k_ref, v_ref, g_ref, s0_ref, o_ref, s_ref = refs
    else:
        lg_ref, q_ref, k_ref, v_ref, g_ref, o_ref, s_ref = refs
    n_blocks = q_ref.shape[0] // rows
    lg = lg_ref[pl.program_id(1)]
    row = lax.broadcasted_iota(jnp.int32, (rows, rows), 0)
    col = lax.broadcasted_iota(jnp.int32, (rows, rows), 1)
    dist = jnp.abs(row - col).astype(F32)
    decay = jnp.where((col // chunk) <= (row // chunk), jnp.exp(lg * dist), 0.0)
    idx = lax.broadcasted_iota(jnp.int32, (rows, 1), 0).astype(F32)
    q_decay = jnp.exp(lg * (idx + 1.0))
    k_decay = jnp.exp(lg * (rows - 1.0 - idx))
    block_decay = jnp.exp(jnp.full((1, 1), rows, F32) * lg)

    if has_init:
        s_ref[...] = s0_ref[...]
    else:
        s_ref[...] = jnp.zeros(s_ref.shape, F32)

    def block(r, carry):
        r0 = pl.multiple_of(r * rows, rows)
        q = q_ref[pl.ds(r0, rows), :]
        k = k_ref[pl.ds(r0, rows), :]
        v = v_ref[pl.ds(r0, rows), :]
        state = s_ref[...]
        scores = _dot_nt(q, k) * decay
        y = _dot(scores.astype(BF16), v) + q_decay * _dot(q, state.astype(BF16))
        kd = (k.astype(F32) * k_decay).astype(BF16)
        s_ref[...] = block_decay * state + _dot_tn(kd, v)
        yn = y * _rms_scale(y, NORM_EPS)
        o_ref[pl.ds(r0, rows), :] = (yn * _silu(g_ref[pl.ds(r0, rows), :])).astype(o_ref.dtype)
        return carry

    lax.fori_loop(0, n_blocks, block, 0)


def _retention(name, log_gamma, q, k, v, g, batch, rows, chunk, init_state=None):
    t = q.shape[0]
    seq = t // batch
    dk = q.shape[1] // B_HEADS
    dv = v.shape[1] // B_HEADS
    in_specs = [pl.BlockSpec(memory_space=pltpu.SMEM),
                pl.BlockSpec((seq, dk), lambda b, h: (b, h)),
                pl.BlockSpec((seq, dk), lambda b, h: (b, h)),
                pl.BlockSpec((seq, dv), lambda b, h: (b, h)),
                pl.BlockSpec((seq, dv), lambda b, h: (b, h))]
    args = [log_gamma, q, k, v, g]
    state_spec = lambda: pl.BlockSpec((None, None, dk, dv), lambda b, h: (b, h, 0, 0))
    if init_state is not None:
        in_specs.append(state_spec())
        args.append(init_state)
    return pl.pallas_call(
        functools.partial(_retention_kernel, rows=rows, chunk=chunk,
                          has_init=init_state is not None),
        grid=(batch, B_HEADS),
        in_specs=in_specs,
        out_specs=[pl.BlockSpec((seq, dv), lambda b, h: (b, h)), state_spec()],
        out_shape=[jax.ShapeDtypeStruct((t, v.shape[1]), BF16),
                   jax.ShapeDtypeStruct((batch, B_HEADS, dk, dv), F32)],
        compiler_params=_compiler_params(2),
        name=name,
    )(*args)


def _rope_tables_a(pos):
    inv = ROPE_THETA ** (-jnp.arange(0, A_ROT_DIM, 2, dtype=F32) / A_ROT_DIM)
    ang = pos[:, None] * inv[None, :]
    cos, sin = jnp.cos(ang), jnp.sin(ang)
    d = jnp.arange(LANES) % A_HEAD_DIM
    first = d < A_ROT_HALF
    second = (d >= A_ROT_HALF) & (d < A_ROT_DIM)
    idx = d % A_ROT_HALF
    cos_t = jnp.where((first | second)[None, :], cos[:, idx], 1.0)
    from_left = jnp.where(second[None, :], sin[:, idx], 0.0)
    from_right = jnp.where(first[None, :], -sin[:, idx], 0.0)
    return cos_t, from_left, from_right


def _xpos_tables_b(pos, dk):
    inv = 1.0 / (XPOS_BASE ** jnp.linspace(0.0, 1.0, dk // 2, dtype=F32))
    ang = pos[:, None] * inv[None, :]
    return jnp.cos(ang), jnp.sin(ang)


def _row_tile(t, cap):
    return cap if t % cap == 0 else t


def kernel(x_prompt, x_sample, cache_k_a, cache_v_a, state_ret, norm_a, w_in_a, lambda_q1,
           lambda_k1, lambda_q2, lambda_k2, subln_a, w_out_a, norm_b, w_in_b, w_out_b,
           norm_final):
    batch, seq, d = x_prompt.shape
    dec_batch, dec_seq, _ = x_sample.shape
    past = cache_k_a.shape[2]
    a_heads = d // (2 * A_HEAD_DIM)
    dk = d // B_HEADS
    dv = 2 * d // B_HEADS
    lambda_init = 0.8 - 0.6 * math.exp(-0.3 * 0)

    xp = x_prompt.reshape(batch * seq, d)
    xs = x_sample.reshape(dec_batch * dec_seq, d)

    w_in_a_b = w_in_a[0].astype(BF16)
    w_out_a_b = w_out_a[0].astype(BF16)
    w_in_b_b = w_in_b[0].astype(BF16)
    w_out_b_b = w_out_b[0].astype(BF16)
    lam_params = jnp.stack([lambda_q1[0], lambda_k1[0], lambda_q2[0], lambda_k2[0]])

    pos_p = jnp.arange(seq, dtype=F32)
    pos_s = jnp.tile(past + jnp.arange(dec_seq, dtype=F32), dec_batch)
    log_gamma = jnp.log1p(-jnp.exp2(-5.0 - jnp.arange(B_HEADS, dtype=F32)))

    tm_p = _row_tile(seq, 512)
    tm_s = dec_batch * dec_seq

    a_widths = [d, d, d, d]
    a_dtypes = [BF16, F32, F32, F32]
    a_kernel = functools.partial(_inproj_a_kernel, q_scale=A_HEAD_DIM ** -0.5)
    qp, kp, vp, gp = _inproj("inproj_a_prompt", a_kernel, xp, norm_a[0], _rope_tables_a(pos_p),
                             w_in_a_b, a_widths, a_dtypes, tm_p, 256)
    qs, ks, vs, gs = _inproj("inproj_a_sample", a_kernel, xs, norm_a[0], _rope_tables_a(pos_s),
                             w_in_a_b, a_widths, a_dtypes, tm_s, 256)
    op = _attn_prompt(qp, kp, vp, gp, lam_params, subln_a[0], batch, seq, lambda_init,
                      tq=min(256, seq))
    os_ = _attn_sample(qs, ks, vs, gs,
                       cache_k_a[0].reshape(dec_batch * past, d),
                       cache_v_a[0].reshape(dec_batch * past, d),
                       lam_params, subln_a[0], dec_batch, lambda_init, heads_per_step=4)
    xp1 = _outproj("outproj_a_prompt", op, w_out_a_b, xp, tm_p)
    xs1 = _outproj("outproj_a_sample", os_, w_out_a_b, xs, tm_s)

    b_widths = [d, d, 2 * d, 2 * d]
    b_dtypes = [BF16, BF16, BF16, F32]
    b_kernel = functools.partial(_inproj_b_kernel, k_scale=dk ** -0.5)
    qp2, kp2, vp2, gp2 = _inproj("inproj_b_prompt", b_kernel, xp1, norm_b[0],
                                 _xpos_tables_b(pos_p, dk), w_in_b_b, b_widths, b_dtypes, tm_p, 256)
    qs2, ks2, vs2, gs2 = _inproj("inproj_b_sample", b_kernel, xs1, norm_b[0],
                                 _xpos_tables_b(pos_s, dk), w_in_b_b, b_widths, b_dtypes, tm_s, 256)
    rp, state_p = _retention("retention_prompt", log_gamma, qp2, kp2, vp2, gp2, batch,
                             rows=min(256, seq), chunk=CHUNK)
    rs, state_s = _retention("retention_sample", log_gamma, qs2, ks2, vs2, gs2, dec_batch,
                             rows=dec_seq, chunk=dec_seq, init_state=state_ret[0])
    yp = _outproj("outproj_b_prompt", rp, w_out_b_b, xp1, tm_p, final_norm_w=norm_final)
    ys = _outproj("outproj_b_sample", rs, w_out_b_b, xs1, tm_s, final_norm_w=norm_final)

    return (yp.reshape(batch, seq, d),
            ys.reshape(dec_batch, dec_seq, d),
            kp.reshape(1, batch, seq, 2 * a_heads, A_HEAD_DIM),
            vp.reshape(1, batch, seq, a_heads, 2 * A_HEAD_DIM),
            state_p[None],
            ks.reshape(1, dec_batch, dec_seq, 2 * a_heads, A_HEAD_DIM),
            vs.reshape(1, dec_batch, dec_seq, a_heads, 2 * A_HEAD_DIM),
            state_s[None])
```

```python
import functools
import math

import jax
import jax.numpy as jnp
from jax import lax
from jax.experimental import pallas as pl
from jax.experimental.pallas import tpu as pltpu

F32 = jnp.float32
BF16 = jnp.bfloat16

CHUNK = 64
A_HEAD_DIM = 64
A_ROT_DIM = A_HEAD_DIM // 4
A_ROT_HALF = A_ROT_DIM // 2
ROPE_THETA = 500000.0
B_HEADS = 8
XPOS_BASE = 10000.0
NORM_EPS = 1e-6
SUBLN_EPS = 1e-5

LANES = 128
SUM_ROWS = 16
LOG2_E = math.log2(math.e)
VMEM_LIMIT_BYTES = 56 * 1024 * 1024

ROW_TILE = 512
PROJ_COL_TILE = 512
ATTN_TILE = 256
RETENTION_ROWS = 256


def _compiler_params(n_axes):
    return pltpu.CompilerParams(
        dimension_semantics=("arbitrary",) * n_axes,
        vmem_limit_bytes=VMEM_LIMIT_BYTES)


def _rms_scale(x, eps):
    return lax.rsqrt(jnp.mean(x * x, axis=-1, keepdims=True) + eps)


def _silu(g):
    h = 0.5 * g
    return h + h * jnp.tanh(h)


def _dot(a, b):
    return jnp.dot(a, b, preferred_element_type=F32)


def _dot_nt(a, b):
    return lax.dot_general(a, b, (((1,), (1,)), ((), ())), preferred_element_type=F32)


def _dot_tn(a, b):
    return lax.dot_general(a, b, (((0,), (0,)), ((), ())), preferred_element_type=F32)


def _normalize_rows(x_ref, nw_ref, xn_ref):
    @pl.when(pl.program_id(1) == 0)
    def _():
        x = x_ref[...]
        xn_ref[...] = (x * _rms_scale(x, NORM_EPS) * nw_ref[...]).astype(BF16)


def _inproj_a_kernel(x_ref, nw_ref, cos_ref, sa_ref, sb_ref, wq_ref, wk_ref, wv_ref, wg_ref,
                     q_ref, k_ref, v_ref, g_ref, xn_ref, *, q_scale):
    _normalize_rows(x_ref, nw_ref, xn_ref)
    xn = xn_ref[...]
    cos, sa, sb = cos_ref[...], sa_ref[...], sb_ref[...]

    def rope_store(z, out_ref, scale):
        for c in range(z.shape[1] // LANES):
            zc = z[:, c * LANES:(c + 1) * LANES]
            r = (zc * cos + pltpu.roll(zc, A_ROT_HALF, 1) * sa
                 + pltpu.roll(zc, LANES - A_ROT_HALF, 1) * sb)
            if scale != 1.0:
                r = r * scale
            out_ref[:, c * LANES:(c + 1) * LANES] = r.astype(out_ref.dtype)

    rope_store(_dot(xn, wq_ref[...]), q_ref, q_scale)
    rope_store(_dot(xn, wk_ref[...]), k_ref, 1.0)
    v_ref[...] = _dot(xn, wv_ref[...])
    g_ref[...] = _dot(xn, wg_ref[...])


def _inproj_b_kernel(x_ref, nw_ref, cos_ref, sin_ref, wq_ref, wk_ref, wv_ref, wg_ref,
                     q_ref, k_ref, v_ref, g_ref, xn_ref, *, k_scale):
    _normalize_rows(x_ref, nw_ref, xn_ref)
    xn = xn_ref[...]
    cos, sin = cos_ref[...], sin_ref[...]

    def rope_store(z, out_ref, scale):
        for hh in range(z.shape[1] // (2 * LANES)):
            c0 = hh * 2 * LANES
            lo = z[:, c0:c0 + LANES]
            hi = z[:, c0 + LANES:c0 + 2 * LANES]
            r_lo = lo * cos - hi * sin
            r_hi = lo * sin + hi * cos
            if scale != 1.0:
                r_lo, r_hi = r_lo * scale, r_hi * scale
            out_ref[:, c0:c0 + LANES] = r_lo.astype(out_ref.dtype)
            out_ref[:, c0 + LANES:c0 + 2 * LANES] = r_hi.astype(out_ref.dtype)

    rope_store(_dot(xn, wq_ref[...]), q_ref, 1.0)
    rope_store(_dot(xn, wk_ref[...]), k_ref, k_scale)
    v_ref[...] = _dot(xn, wv_ref[...]).astype(v_ref.dtype)
    g_ref[...] = _dot(xn, wg_ref[...])


def _inproj(name, kernel_fn, x, norm_w, tables, w_bf16, widths, out_dtypes, tm, tn_unit):
    t, d = x.shape
    nj = min(widths) // tn_unit
    tns = [w // nj for w in widths]
    offs = [sum(widths[:s]) // tns[s] for s in range(len(widths))]
    n_pos_blocks = tables[0].shape[0] // tm

    in_specs = [pl.BlockSpec((tm, d), lambda i, j: (i, 0)),
                pl.BlockSpec((1, d), lambda i, j: (0, 0))]
    in_specs += [pl.BlockSpec((tm, LANES), lambda i, j: (i % n_pos_blocks, 0)) for _ in tables]
    for s in range(len(widths)):
        in_specs.append(pl.BlockSpec((d, tns[s]), functools.partial(
            lambda i, j, off: (0, off + j), off=offs[s])))
    out_specs = [pl.BlockSpec((tm, tns[s]), lambda i, j: (i, j)) for s in range(len(widths))]
    out_shape = [jax.ShapeDtypeStruct((t, widths[s]), out_dtypes[s]) for s in range(len(widths))]
    return pl.pallas_call(
        kernel_fn,
        grid=(t // tm, nj),
        in_specs=in_specs,
        out_specs=out_specs,
        out_shape=out_shape,
        scratch_shapes=[pltpu.VMEM((tm, d), BF16)],
        compiler_params=_compiler_params(2),
        name=name,
    )(x, norm_w.reshape(1, d), *tables, *([w_bf16] * len(widths)))


def _lambda_value(lam_ref, lambda_init):
    lp = lam_ref[...]
    s1 = jnp.sum(lp[0:1] * lp[1:2], axis=-1, keepdims=True)
    s2 = jnp.sum(lp[2:3] * lp[3:4], axis=-1, keepdims=True)
    return jnp.exp(s1) - jnp.exp(s2) + lambda_init


def _split_subheads(q):
    lane = lax.broadcasted_iota(jnp.int32, q.shape, 1)
    qf = q.astype(F32)
    return jnp.where(lane < A_HEAD_DIM, qf, 0.0), jnp.where(lane >= A_HEAD_DIM, qf, 0.0)


def _diff_finish(o1, o2, lam, sg, g, lambda_init):
    o = o1 - lam * o2
    on = o * _rms_scale(o, SUBLN_EPS) * sg
    on = on * (1.0 - lambda_init)
    return on * _silu(g)


def _attn_prompt_kernel(q_ref, k_ref, v_ref, g_ref, lam_ref, sg_ref, o_ref,
                        kb_ref, vt_ref, qt_ref, m_ref, acc_ref, s_ref, p_ref, a_ref,
                        *, tq, lambda_init):
    seq = q_ref.shape[0]
    n_tiles = seq // tq
    kb_ref[...] = k_ref[...].astype(BF16)
    ones_row = lax.broadcasted_iota(jnp.int32, (SUM_ROWS, tq), 0) == 0
    for c in range(n_tiles):
        vt_ref[c, 0:LANES] = v_ref[c * tq:(c + 1) * tq, :].T.astype(BF16)
        vt_ref[c, LANES:LANES + SUM_ROWS] = jnp.where(ones_row, 1.0, 0.0).astype(BF16)
    lam = _lambda_value(lam_ref, lambda_init)
    sg = sg_ref[...]
    for c in range(n_tiles):
        qa, qb = _split_subheads(q_ref[c * tq:(c + 1) * tq, :])
        qt_ref[c, 0] = qa.T.astype(BF16)
        qt_ref[c, 1] = qb.T.astype(BF16)
    m_ref[...] = jnp.full(m_ref.shape, -jnp.inf, F32)
    acc_ref[...] = jnp.zeros(acc_ref.shape, F32)
    key = lax.broadcasted_iota(jnp.int32, (tq, tq), 0)
    qry = lax.broadcasted_iota(jnp.int32, (tq, tq), 1)
    diag_mask = (key // CHUNK) <= (qry // CHUNK)

    items = [(qi, kt) for qi in range(n_tiles) for kt in range(qi)]
    items += [(qi, qi) for qi in range(n_tiles)]

    def score_stage(j):
        qi, kt = items[j]
        for i in range(2):
            s_ref[j % 2, i] = _dot(kb_ref[kt * tq:(kt + 1) * tq, :], qt_ref[qi, i])

    def softmax_stage(j):
        qi, kt = items[j]
        for i in range(2):
            s = s_ref[j % 2, i]
            if kt == qi:
                s = jnp.where(diag_mask, s, -jnp.inf)
            m_old = m_ref[qi, i]
            m_new = jnp.maximum(m_old, jnp.max(s, axis=0, keepdims=True))
            m_ref[qi, i] = m_new
            a_ref[j % 2, i] = jnp.exp2(m_old - m_new)
            p_ref[j % 2, i] = jnp.exp2(s - m_new).astype(BF16)

    def value_stage(j):
        qi, kt = items[j]
        for i in range(2):
            acc_ref[qi, i] = (a_ref[j % 2, i] * acc_ref[qi, i]
                              + _dot(vt_ref[kt], p_ref[j % 2, i]))

    for step in range(len(items) + 2):
        if 0 <= step - 2:
            value_stage(step - 2)
        if 0 <= step - 1 < len(items):
            softmax_stage(step - 1)
        if step < len(items):
            score_stage(step)

    for qi in range(n_tiles):
        o1, o2 = [(acc_ref[qi, i, 0:LANES] * (1.0 / acc_ref[qi, i, LANES:LANES + 1])).T
                  for i in range(2)]
        out = _diff_finish(o1, o2, lam, sg, g_ref[qi * tq:(qi + 1) * tq, :], lambda_init)
        o_ref[qi * tq:(qi + 1) * tq, :] = out.astype(o_ref.dtype)


def _attn_prompt(q, k, v, g, lam_params, subln, batch, seq, lambda_init, tq):
    t, width = q.shape
    heads = width // LANES
    n_tiles = seq // tq
    blk = lambda: pl.BlockSpec((seq, LANES), lambda b, h: (b, h))
    return pl.pallas_call(
        functools.partial(_attn_prompt_kernel, tq=tq, lambda_init=lambda_init),
        grid=(batch, heads),
        in_specs=[blk(), blk(), blk(), blk(),
                  pl.BlockSpec(lam_params.shape, lambda b, h: (0, 0)),
                  pl.BlockSpec((1, LANES), lambda b, h: (0, 0))],
        out_specs=blk(),
        out_shape=jax.ShapeDtypeStruct((t, width), BF16),
        scratch_shapes=[pltpu.VMEM((seq, LANES), BF16),
                        pltpu.VMEM((n_tiles, LANES + SUM_ROWS, tq), BF16),
                        pltpu.VMEM((n_tiles, 2, LANES, tq), BF16),
                        pltpu.VMEM((n_tiles, 2, 1, tq), F32),
                        pltpu.VMEM((n_tiles, 2, LANES + SUM_ROWS, tq), F32),
                        pltpu.VMEM((2, 2, tq, tq), F32),
                        pltpu.VMEM((2, 2, tq, tq), BF16),
                        pltpu.VMEM((2, 2, 1, tq), F32)],
        compiler_params=_compiler_params(2),
        name="attn_prompt",
    )(q, k, v, g, lam_params, subln.reshape(1, LANES))


def _attn_sample_kernel(q_ref, kn_ref, vn_ref, ck_ref, cv_ref, g_ref, lam_ref, sg_ref, o_ref,
                        *, heads_per_step, lambda_init):
    lam = _lambda_value(lam_ref, lambda_init)
    sg = sg_ref[...]
    for hh in range(heads_per_step):
        cs = slice(hh * LANES, (hh + 1) * LANES)
        qs = [x.astype(BF16) for x in _split_subheads(q_ref[:, cs])]
        kc = ck_ref[:, cs].astype(BF16)
        vc = cv_ref[:, cs].astype(BF16)
        kn = kn_ref[:, cs].astype(BF16)
        vn = vn_ref[:, cs].astype(BF16)
        outs = []
        for i in range(2):
            sc = _dot_nt(qs[i], kc)
            sn = _dot_nt(qs[i], kn)
            m = jnp.maximum(jnp.max(sc, axis=-1, keepdims=True),
                            jnp.max(sn, axis=-1, keepdims=True))
            pc = jnp.exp2(sc - m)
            pn = jnp.exp2(sn - m)
            l = jnp.sum(pc, axis=-1, keepdims=True) + jnp.sum(pn, axis=-1, keepdims=True)
            acc = _dot(pc.astype(BF16), vc) + _dot(pn.astype(BF16), vn)
            outs.append(acc * (1.0 / l))
        out = _diff_finish(outs[0], outs[1], lam, sg, g_ref[:, cs], lambda_init)
        o_ref[:, cs] = out.astype(o_ref.dtype)


def _attn_sample(q, k, v, g, cache_k, cache_v, lam_params, subln, batch, lambda_init,
                 heads_per_step):
    t, width = q.shape
    frames = t // batch
    past = cache_k.shape[0] // batch
    cols = heads_per_step * LANES
    new = lambda: pl.BlockSpec((frames, cols), lambda b, h: (b, h))
    old = lambda: pl.BlockSpec((past, cols), lambda b, h: (b, h))
    return pl.pallas_call(
        functools.partial(_attn_sample_kernel, heads_per_step=heads_per_step,
                          lambda_init=lambda_init),
        grid=(batch, width // cols),
        in_specs=[new(), new(), new(), old(), old(), new(),
                  pl.BlockSpec(lam_params.shape, lambda b, h: (0, 0)),
                  pl.BlockSpec((1, LANES), lambda b, h: (0, 0))],
        out_specs=new(),
        out_shape=jax.ShapeDtypeStruct((t, width), BF16),
        compiler_params=_compiler_params(2),
        name="attn_sample",
    )(q, k, v, cache_k, cache_v, g, lam_params, subln.reshape(1, LANES))


def _outproj_kernel(a_ref, w_ref, x_ref, y_ref):
    y_ref[...] = x_ref[...] + _dot(a_ref[...], w_ref[...])


def _outproj_final_kernel(a_ref, w_ref, x_ref, nf_ref, y_ref):
    xo = x_ref[...] + _dot(a_ref[...], w_ref[...])
    y_ref[...] = xo * _rms_scale(xo, NORM_EPS) * nf_ref[...]


def _outproj(name, a, w_bf16, x, tm, final_norm_w=None):
    t, kdim = a.shape
    d = x.shape[1]
    in_specs = [pl.BlockSpec((tm, kdim), lambda i: (i, 0)),
                pl.BlockSpec((kdim, d), lambda i: (0, 0), pipeline_mode=pl.Buffered(1)),
                pl.BlockSpec((tm, d), lambda i: (i, 0))]
    args = [a, w_bf16, x]
    kernel_fn = _outproj_kernel
    if final_norm_w is not None:
        in_specs.append(pl.BlockSpec((1, d), lambda i: (0, 0)))
        args.append(final_norm_w.reshape(1, d))
        kernel_fn = _outproj_final_kernel
    return pl.pallas_call(
        kernel_fn,
        grid=(t // tm,),
        in_specs=in_specs,
        out_specs=pl.BlockSpec((tm, d), lambda i: (i, 0)),
        out_shape=jax.ShapeDtypeStruct((t, d), F32),
        compiler_params=_compiler_params(1),
        name=name,
    )(*args)


def _retention_kernel(*refs, rows, chunk, has_init):
    if has_init:
        lg_ref, q_ref, k_ref, v_ref, g_ref, s0_ref, o_ref, s_ref, y_ref = refs
    else:
        lg_ref, q_ref, k_ref, v_ref, g_ref, o_ref, s_ref, y_ref = refs
    n_blocks = q_ref.shape[0] // rows
    lg = lg_ref[pl.program_id(1)]
    row = lax.broadcasted_iota(jnp.int32, (rows, rows), 0)
    col = lax.broadcasted_iota(jnp.int32, (rows, rows), 1)
    dist = jnp.abs(row - col).astype(F32)
    decay = jnp.where((col // chunk) <= (row // chunk), jnp.exp(lg * dist), 0.0)
    idx = lax.broadcasted_iota(jnp.int32, (rows, 1), 0).astype(F32)
    q_decay = jnp.exp(lg * (idx + 1.0))
    k_decay = jnp.exp(lg * (rows - 1.0 - idx))
    block_decay = jnp.exp(jnp.full((1, 1), rows, F32) * lg)

    if has_init:
        s_ref[...] = s0_ref[...]
    else:
        s_ref[...] = jnp.zeros(s_ref.shape, F32)

    def mix_block(r):
        sl = slice(r * rows, (r + 1) * rows)
        q, k, v = q_ref[sl, :], k_ref[sl, :], v_ref[sl, :]
        state = s_ref[...]
        scores = _dot_nt(q, k) * decay
        y_ref[r % 2] = _dot(scores.astype(BF16), v) + q_decay * _dot(q, state.astype(BF16))
        kd = (k.astype(F32) * k_decay).astype(BF16)
        s_ref[...] = block_decay * state + _dot_tn(kd, v)

    def gate_block(r):
        sl = slice(r * rows, (r + 1) * rows)
        y = y_ref[r % 2]
        yn = y * _rms_scale(y, NORM_EPS)
        o_ref[sl, :] = (yn * _silu(g_ref[sl, :])).astype(o_ref.dtype)

    for r in range(n_blocks + 1):
        if r < n_blocks:
            mix_block(r)
        if r >= 1:
            gate_block(r - 1)


def _retention(name, log_gamma, q, k, v, g, batch, rows, chunk, init_state=None):
    t = q.shape[0]
    seq = t // batch
    dk = q.shape[1] // B_HEADS
    dv = v.shape[1] // B_HEADS
    in_specs = [pl.BlockSpec(memory_space=pltpu.SMEM),
                pl.BlockSpec((seq, dk), lambda b, h: (b, h)),
                pl.BlockSpec((seq, dk), lambda b, h: (b, h)),
                pl.BlockSpec((seq, dv), lambda b, h: (b, h)),
                pl.BlockSpec((seq, dv), lambda b, h: (b, h))]
    args = [log_gamma, q, k, v, g]
    state_spec = lambda: pl.BlockSpec((None, None, dk, dv), lambda b, h: (b, h, 0, 0))
    if init_state is not None:
        in_specs.append(state_spec())
        args.append(init_state)
    return pl.pallas_call(
        functools.partial(_retention_kernel, rows=rows, chunk=chunk,
                          has_init=init_state is not None),
        grid=(batch, B_HEADS),
        in_specs=in_specs,
        out_specs=[pl.BlockSpec((seq, dv), lambda b, h: (b, h)), state_spec()],
        out_shape=[jax.ShapeDtypeStruct((t, v.shape[1]), BF16),
                   jax.ShapeDtypeStruct((batch, B_HEADS, dk, dv), F32)],
        scratch_shapes=[pltpu.VMEM((2, rows, dv), F32)],
        compiler_params=_compiler_params(2),
        name=name,
    )(*args)


def _rope_tables_a(pos):
    inv = ROPE_THETA ** (-jnp.arange(0, A_ROT_DIM, 2, dtype=F32) / A_ROT_DIM)
    ang = pos[:, None] * inv[None, :]
    cos, sin = jnp.cos(ang), jnp.sin(ang)
    d = jnp.arange(LANES) % A_HEAD_DIM
    first = d < A_ROT_HALF
    second = (d >= A_ROT_HALF) & (d < A_ROT_DIM)
    idx = d % A_ROT_HALF
    cos_t = jnp.where((first | second)[None, :], cos[:, idx], 1.0)
    from_left = jnp.where(second[None, :], sin[:, idx], 0.0)
    from_right = jnp.where(first[None, :], -sin[:, idx], 0.0)
    return cos_t, from_left, from_right


def _xpos_tables_b(pos, dk):
    inv = 1.0 / (XPOS_BASE ** jnp.linspace(0.0, 1.0, dk // 2, dtype=F32))
    ang = pos[:, None] * inv[None, :]
    return jnp.cos(ang), jnp.sin(ang)


def _row_tile(t, cap):
    return cap if t % cap == 0 else t


def kernel(x_prompt, x_sample, cache_k_a, cache_v_a, state_ret, norm_a, w_in_a, lambda_q1,
           lambda_k1, lambda_q2, lambda_k2, subln_a, w_out_a, norm_b, w_in_b, w_out_b,
           norm_final):
    batch, seq, d = x_prompt.shape
    dec_batch, dec_seq, _ = x_sample.shape
    past = cache_k_a.shape[2]
    a_heads = d // (2 * A_HEAD_DIM)
    dk = d // B_HEADS
    dv = 2 * d // B_HEADS
    lambda_init = 0.8 - 0.6 * math.exp(-0.3 * 0)

    xp = x_prompt.reshape(batch * seq, d)
    xs = x_sample.reshape(dec_batch * dec_seq, d)

    w_in_a_b = w_in_a[0].astype(BF16)
    w_out_a_b = w_out_a[0].astype(BF16)
    w_in_b_b = w_in_b[0].astype(BF16)
    w_out_b_b = w_out_b[0].astype(BF16)
    lam_params = jnp.stack([lambda_q1[0], lambda_k1[0], lambda_q2[0], lambda_k2[0]])

    pos_p = jnp.arange(seq, dtype=F32)
    pos_s = jnp.tile(past + jnp.arange(dec_seq, dtype=F32), dec_batch)
    log_gamma = jnp.log1p(-jnp.exp2(-5.0 - jnp.arange(B_HEADS, dtype=F32)))

    tm_p = _row_tile(seq, ROW_TILE)
    tm_s = dec_batch * dec_seq
    tn = PROJ_COL_TILE

    a_widths = [d, d, d, d]
    a_dtypes = [BF16, F32, F32, F32]
    a_kernel = functools.partial(_inproj_a_kernel, q_scale=A_HEAD_DIM ** -0.5 * LOG2_E)
    qp, kp, vp, gp = _inproj("inproj_a_prompt", a_kernel, xp, norm_a[0], _rope_tables_a(pos_p),
                             w_in_a_b, a_widths, a_dtypes, tm_p, tn)
    qs, ks, vs, gs = _inproj("inproj_a_sample", a_kernel, xs, norm_a[0], _rope_tables_a(pos_s),
                             w_in_a_b, a_widths, a_dtypes, tm_s, tn)
    op = _attn_prompt(qp, kp, vp, gp, lam_params, subln_a[0], batch, seq, lambda_init,
                      tq=min(ATTN_TILE, seq))
    os_ = _attn_sample(qs, ks, vs, gs,
                       cache_k_a[0].reshape(dec_batch * past, d),
                       cache_v_a[0].reshape(dec_batch * past, d),
                       lam_params, subln_a[0], dec_batch, lambda_init, heads_per_step=4)
    xp1 = _outproj("outproj_a_prompt", op, w_out_a_b, xp, tm_p)
    xs1 = _outproj("outproj_a_sample", os_, w_out_a_b, xs, tm_s)

    b_widths = [d, d, 2 * d, 2 * d]
    b_dtypes = [BF16, BF16, BF16, F32]
    b_kernel = functools.partial(_inproj_b_kernel, k_scale=dk ** -0.5)
    qp2, kp2, vp2, gp2 = _inproj("inproj_b_prompt", b_kernel, xp1, norm_b[0],
                                 _xpos_tables_b(pos_p, dk), w_in_b_b, b_widths, b_dtypes, tm_p, tn)
    qs2, ks2, vs2, gs2 = _inproj("inproj_b_sample", b_kernel, xs1, norm_b[0],
                                 _xpos_tables_b(pos_s, dk), w_in_b_b, b_widths, b_dtypes, tm_s, tn)
    rp, state_p = _retention("retention_prompt", log_gamma, qp2, kp2, vp2, gp2, batch,
                             rows=min(RETENTION_ROWS, seq), chunk=CHUNK)
    rs, state_s = _retention("retention_sample", log_gamma, qs2, ks2, vs2, gs2, dec_batch,
                             rows=dec_seq, chunk=dec_seq, init_state=state_ret[0])
    yp = _outproj("outproj_b_prompt", rp, w_out_b_b, xp1, tm_p, final_norm_w=norm_final)
    ys = _outproj("outproj_b_sample", rs, w_out_b_b, xs1, tm_s, final_norm_w=norm_final)

    return (yp.reshape(batch, seq, d),
            ys.reshape(dec_batch, dec_seq, d),
            kp.reshape(1, batch, seq, 2 * a_heads, A_HEAD_DIM),
            vp.reshape(1, batch, seq, a_heads, 2 * A_HEAD_DIM),
            state_p[None],
            ks.reshape(1, dec_batch, dec_seq, 2 * a_heads, A_HEAD_DIM),
            vs.reshape(1, dec_batch, dec_seq, a_heads, 2 * A_HEAD_DIM),
            state_s[None])
```

```python
import functools
import math

import jax
import jax.numpy as jnp
from jax import lax
from jax.experimental import pallas as pl
from jax.experimental.pallas import tpu as pltpu

F32 = jnp.float32
BF16 = jnp.bfloat16

CHUNK = 64
A_HEAD_DIM = 64
A_ROT_DIM = A_HEAD_DIM // 4
A_ROT_HALF = A_ROT_DIM // 2
ROPE_THETA = 500000.0
B_HEADS = 8
XPOS_BASE = 10000.0
NORM_EPS = 1e-6
SUBLN_EPS = 1e-5

LANES = 128
SUM_ROWS = 16
LOG2_E = math.log2(math.e)
VMEM_LIMIT_BYTES = 56 * 1024 * 1024

ROW_TILE = 512
PROJ_COL_TILE = 512
ATTN_TILE = 256
RETENTION_ROWS = 256


def _compiler_params(n_axes):
    return pltpu.CompilerParams(
        dimension_semantics=("arbitrary",) * n_axes,
        vmem_limit_bytes=VMEM_LIMIT_BYTES)


def _rms_scale(x, eps):
    return lax.rsqrt(jnp.mean(x * x, axis=-1, keepdims=True) + eps)


def _silu(g):
    h = 0.5 * g
    return h + h * jnp.tanh(h)


def _dot(a, b):
    return jnp.dot(a, b, preferred_element_type=F32)


def _dot_nt(a, b):
    return lax.dot_general(a, b, (((1,), (1,)), ((), ())), preferred_element_type=F32)


def _dot_tn(a, b):
    return lax.dot_general(a, b, (((0,), (0,)), ((), ())), preferred_element_type=F32)


def _normalize_rows(x_ref, nw_ref, xn_ref):
    @pl.when(pl.program_id(1) == 0)
    def _():
        x = x_ref[...]
        xn_ref[...] = (x * _rms_scale(x, NORM_EPS) * nw_ref[...]).astype(BF16)


def _inproj_a_kernel(x_ref, nw_ref, cos_ref, sa_ref, sb_ref, wq_ref, wk_ref, wv_ref, wg_ref,
                     q_ref, k_ref, v_ref, g_ref, kh_ref, xn_ref, *, q_scale):
    _normalize_rows(x_ref, nw_ref, xn_ref)
    xn = xn_ref[...]
    cos, sa, sb = cos_ref[...], sa_ref[...], sb_ref[...]

    def rope_store(z, out_ref, scale, heads_ref=None):
        for c in range(z.shape[1] // LANES):
            zc = z[:, c * LANES:(c + 1) * LANES]
            r = (zc * cos + pltpu.roll(zc, A_ROT_HALF, 1) * sa
                 + pltpu.roll(zc, LANES - A_ROT_HALF, 1) * sb)
            if scale != 1.0:
                r = r * scale
            out_ref[:, c * LANES:(c + 1) * LANES] = r.astype(out_ref.dtype)
            if heads_ref is not None:
                heads_ref[:, 2 * c, :] = r[:, :A_HEAD_DIM]
                heads_ref[:, 2 * c + 1, :] = pltpu.roll(r, A_HEAD_DIM, 1)[:, :A_HEAD_DIM]

    rope_store(_dot(xn, wq_ref[...]), q_ref, q_scale)
    rope_store(_dot(xn, wk_ref[...]), k_ref, 1.0, kh_ref)
    v_ref[...] = _dot(xn, wv_ref[...])
    g_ref[...] = _dot(xn, wg_ref[...]).astype(g_ref.dtype)


def _inproj_b_kernel(x_ref, nw_ref, cos_ref, sin_ref, wq_ref, wk_ref, wv_ref, wg_ref,
                     q_ref, k_ref, v_ref, g_ref, xn_ref, *, k_scale):
    _normalize_rows(x_ref, nw_ref, xn_ref)
    xn = xn_ref[...]
    cos, sin = cos_ref[...], sin_ref[...]

    def rope_store(z, out_ref, scale):
        for hh in range(z.shape[1] // (2 * LANES)):
            c0 = hh * 2 * LANES
            lo = z[:, c0:c0 + LANES]
            hi = z[:, c0 + LANES:c0 + 2 * LANES]
            r_lo = lo * cos - hi * sin
            r_hi = lo * sin + hi * cos
            if scale != 1.0:
                r_lo, r_hi = r_lo * scale, r_hi * scale
            out_ref[:, c0:c0 + LANES] = r_lo.astype(out_ref.dtype)
            out_ref[:, c0 + LANES:c0 + 2 * LANES] = r_hi.astype(out_ref.dtype)

    rope_store(_dot(xn, wq_ref[...]), q_ref, 1.0)
    rope_store(_dot(xn, wk_ref[...]), k_ref, k_scale)
    v_ref[...] = _dot(xn, wv_ref[...]).astype(v_ref.dtype)
    g_ref[...] = _dot(xn, wg_ref[...]).astype(g_ref.dtype)


def _inproj(name, kernel_fn, x, norm_w, tables, w_bf16, widths, out_dtypes, tm, tn_unit,
            k_by_head=False):
    t, d = x.shape
    nj = min(widths) // tn_unit
    tns = [w // nj for w in widths]
    offs = [sum(widths[:s]) // tns[s] for s in range(len(widths))]
    n_pos_blocks = tables[0].shape[0] // tm

    in_specs = [pl.BlockSpec((tm, d), lambda i, j: (i, 0)),
                pl.BlockSpec((1, d), lambda i, j: (0, 0))]
    in_specs += [pl.BlockSpec((tm, LANES), lambda i, j: (i % n_pos_blocks, 0)) for _ in tables]
    for s in range(len(widths)):
        in_specs.append(pl.BlockSpec((d, tns[s]), functools.partial(
            lambda i, j, off: (0, off + j), off=offs[s])))
    out_specs = [pl.BlockSpec((tm, tns[s]), lambda i, j: (i, j)) for s in range(len(widths))]
    out_shape = [jax.ShapeDtypeStruct((t, widths[s]), out_dtypes[s]) for s in range(len(widths))]
    if k_by_head:
        out_specs.append(pl.BlockSpec((tm, tns[1] // A_HEAD_DIM, A_HEAD_DIM), lambda i, j: (i, j, 0)))
        out_shape.append(jax.ShapeDtypeStruct((t, widths[1] // A_HEAD_DIM, A_HEAD_DIM), F32))
    return pl.pallas_call(
        kernel_fn,
        grid=(t // tm, nj),
        in_specs=in_specs,
        out_specs=out_specs,
        out_shape=out_shape,
        scratch_shapes=[pltpu.VMEM((tm, d), BF16)],
        compiler_params=_compiler_params(2),
        name=name,
    )(x, norm_w.reshape(1, d), *tables, *([w_bf16] * len(widths)))


def _lambda_value(lam_ref, lambda_init):
    lp = lam_ref[...]
    s1 = jnp.sum(lp[0:1] * lp[1:2], axis=-1, keepdims=True)
    s2 = jnp.sum(lp[2:3] * lp[3:4], axis=-1, keepdims=True)
    return jnp.exp(s1) - jnp.exp(s2) + lambda_init


def _split_subheads(q):
    lane = lax.broadcasted_iota(jnp.int32, q.shape, 1)
    qf = q.astype(F32)
    return jnp.where(lane < A_HEAD_DIM, qf, 0.0), jnp.where(lane >= A_HEAD_DIM, qf, 0.0)


def _diff_finish(o1, o2, lam, sg, g, lambda_init):
    o = o1 - lam * o2
    on = o * _rms_scale(o, SUBLN_EPS) * sg
    on = on * (1.0 - lambda_init)
    return on * _silu(g)


def _attn_prompt_kernel(q_ref, k_ref, v_ref, g_ref, lam_ref, sg_ref, o_ref,
                        kb_ref, vt_ref, qt_ref, m_ref, acc_ref, s_ref, p_ref, a_ref,
                        *, tq, lambda_init):
    seq = q_ref.shape[0]
    n_tiles = seq // tq
    kb_ref[...] = k_ref[...].astype(BF16)
    ones_row = lax.broadcasted_iota(jnp.int32, (SUM_ROWS, tq), 0) == 0
    for c in range(n_tiles):
        vt_ref[c, 0:LANES] = v_ref[c * tq:(c + 1) * tq, :].T.astype(BF16)
        vt_ref[c, LANES:LANES + SUM_ROWS] = jnp.where(ones_row, 1.0, 0.0).astype(BF16)
    lam = _lambda_value(lam_ref, lambda_init)
    sg = sg_ref[...]
    for c in range(n_tiles):
        qa, qb = _split_subheads(q_ref[c * tq:(c + 1) * tq, :])
        qt_ref[c, 0] = qa.T.astype(BF16)
        qt_ref[c, 1] = qb.T.astype(BF16)
    m_ref[...] = jnp.full(m_ref.shape, -jnp.inf, F32)
    acc_ref[...] = jnp.zeros(acc_ref.shape, F32)
    key = lax.broadcasted_iota(jnp.int32, (tq, tq), 0)
    qry = lax.broadcasted_iota(jnp.int32, (tq, tq), 1)
    diag_mask = (key // CHUNK) <= (qry // CHUNK)

    items = [(qi, kt) for qi in range(n_tiles) for kt in range(qi)]
    items += [(qi, qi) for qi in range(n_tiles)]

    def score_stage(j):
        qi, kt = items[j]
        for i in range(2):
            s_ref[j % 2, i] = _dot(kb_ref[kt * tq:(kt + 1) * tq, :], qt_ref[qi, i])

    def softmax_stage(j):
        qi, kt = items[j]
        for i in range(2):
            s = s_ref[j % 2, i]
            if kt == qi:
                s = jnp.where(diag_mask, s, -jnp.inf)
            m_old = m_ref[qi, i]
            m_new = jnp.maximum(m_old, jnp.max(s, axis=0, keepdims=True))
            m_ref[qi, i] = m_new
            a_ref[j % 2, i] = jnp.exp2(m_old - m_new)
            p_ref[j % 2, i] = jnp.exp2(s - m_new).astype(BF16)

    def value_stage(j):
        qi, kt = items[j]
        for i in range(2):
            acc_ref[qi, i] = (a_ref[j % 2, i] * acc_ref[qi, i]
                              + _dot(vt_ref[kt], p_ref[j % 2, i]))

    for step in range(len(items) + 2):
        if 0 <= step - 2:
            value_stage(step - 2)
        if 0 <= step - 1 < len(items):
            softmax_stage(step - 1)
        if step < len(items):
            score_stage(step)

    for qi in range(n_tiles):
        o1, o2 = [(acc_ref[qi, i, 0:LANES] * (1.0 / acc_ref[qi, i, LANES:LANES + 1])).T
                  for i in range(2)]
        out = _diff_finish(o1, o2, lam, sg, g_ref[qi * tq:(qi + 1) * tq, :].astype(F32),
                           lambda_init)
        o_ref[qi * tq:(qi + 1) * tq, :] = out.astype(o_ref.dtype)


def _attn_prompt(q, k, v, g, lam_params, subln, batch, seq, lambda_init, tq):
    t, width = q.shape
    heads = width // LANES
    n_tiles = seq // tq
    blk = lambda: pl.BlockSpec((seq, LANES), lambda b, h: (b, h))
    return pl.pallas_call(
        functools.partial(_attn_prompt_kernel, tq=tq, lambda_init=lambda_init),
        grid=(batch, heads),
        in_specs=[blk(), blk(), blk(), blk(),
                  pl.BlockSpec(lam_params.shape, lambda b, h: (0, 0)),
                  pl.BlockSpec((1, LANES), lambda b, h: (0, 0))],
        out_specs=blk(),
        out_shape=jax.ShapeDtypeStruct((t, width), BF16),
        scratch_shapes=[pltpu.VMEM((seq, LANES), BF16),
                        pltpu.VMEM((n_tiles, LANES + SUM_ROWS, tq), BF16),
                        pltpu.VMEM((n_tiles, 2, LANES, tq), BF16),
                        pltpu.VMEM((n_tiles, 2, 1, tq), F32),
                        pltpu.VMEM((n_tiles, 2, LANES + SUM_ROWS, tq), F32),
                        pltpu.VMEM((2, 2, tq, tq), F32),
                        pltpu.VMEM((2, 2, tq, tq), BF16),
                        pltpu.VMEM((2, 2, 1, tq), F32)],
        compiler_params=_compiler_params(2),
        name="attn_prompt",
    )(q, k, v, g, lam_params, subln.reshape(1, LANES))


def _attn_sample_kernel(q_ref, kn_ref, vn_ref, ck_ref, cv_ref, g_ref, lam_ref, sg_ref, o_ref,
                        *, heads_per_step, lambda_init):
    lam = _lambda_value(lam_ref, lambda_init)
    sg = sg_ref[...]
    for hh in range(heads_per_step):
        cs = slice(hh * LANES, (hh + 1) * LANES)
        qs = jnp.concatenate(_split_subheads(q_ref[:, cs]), axis=0).astype(BF16)
        kc = ck_ref[:, cs].astype(BF16)
        vc = cv_ref[:, cs].astype(BF16)
        kn = kn_ref[:, cs].astype(BF16)
        vn = vn_ref[:, cs].astype(BF16)
        sc = _dot_nt(qs, kc)
        sn = _dot_nt(qs, kn)
        m = jnp.maximum(jnp.max(sc, axis=-1, keepdims=True),
                        jnp.max(sn, axis=-1, keepdims=True))
        pc = jnp.exp2(sc - m)
        pn = jnp.exp2(sn - m)
        l = jnp.sum(pc, axis=-1, keepdims=True) + jnp.sum(pn, axis=-1, keepdims=True)
        acc = _dot(pc.astype(BF16), vc) + _dot(pn.astype(BF16), vn)
        o12 = acc * (1.0 / l)
        frames = q_ref.shape[0]
        out = _diff_finish(o12[:frames], o12[frames:], lam, sg, g_ref[:, cs].astype(F32),
                           lambda_init)
        o_ref[:, cs] = out.astype(o_ref.dtype)


def _attn_sample(q, k, v, g, cache_k, cache_v, lam_params, subln, batch, lambda_init,
                 heads_per_step):
    t, width = q.shape
    frames = t // batch
    past = cache_k.shape[0] // batch
    cols = heads_per_step * LANES
    new = lambda: pl.BlockSpec((frames, cols), lambda b, h: (b, h))
    old = lambda: pl.BlockSpec((past, cols), lambda b, h: (b, h))
    return pl.pallas_call(
        functools.partial(_attn_sample_kernel, heads_per_step=heads_per_step,
                          lambda_init=lambda_init),
        grid=(batch, width // cols),
        in_specs=[new(), new(), new(), old(), old(), new(),
                  pl.BlockSpec(lam_params.shape, lambda b, h: (0, 0)),
                  pl.BlockSpec((1, LANES), lambda b, h: (0, 0))],
        out_specs=new(),
        out_shape=jax.ShapeDtypeStruct((t, width), BF16),
        compiler_params=_compiler_params(2),
        name="attn_sample",
    )(q, k, v, cache_k, cache_v, g, lam_params, subln.reshape(1, LANES))


def _outproj_kernel(a_ref, w_ref, x_ref, y_ref):
    y_ref[...] = x_ref[...] + _dot(a_ref[...], w_ref[...])


def _outproj_final_kernel(a_ref, w_ref, x_ref, nf_ref, y_ref):
    xo = x_ref[...] + _dot(a_ref[...], w_ref[...])
    y_ref[...] = xo * _rms_scale(xo, NORM_EPS) * nf_ref[...]


def _outproj(name, a, w_bf16, x, tm, final_norm_w=None):
    t, kdim = a.shape
    d = x.shape[1]
    in_specs = [pl.BlockSpec((tm, kdim), lambda i: (i, 0)),
                pl.BlockSpec((kdim, d), lambda i: (0, 0), pipeline_mode=pl.Buffered(1)),
                pl.BlockSpec((tm, d), lambda i: (i, 0))]
    args = [a, w_bf16, x]
    kernel_fn = _outproj_kernel
    if final_norm_w is not None:
        in_specs.append(pl.BlockSpec((1, d), lambda i: (0, 0)))
        args.append(final_norm_w.reshape(1, d))
        kernel_fn = _outproj_final_kernel
    return pl.pallas_call(
        kernel_fn,
        grid=(t // tm,),
        in_specs=in_specs,
        out_specs=pl.BlockSpec((tm, d), lambda i: (i, 0)),
        out_shape=jax.ShapeDtypeStruct((t, d), F32),
        compiler_params=_compiler_params(1),
        name=name,
    )(*args)


def _retention_kernel(*refs, rows, chunk, has_init):
    if has_init:
        lg_ref, q_ref, k_ref, v_ref, g_ref, s0_ref, o_ref, s_ref, y_ref = refs
    else:
        lg_ref, q_ref, k_ref, v_ref, g_ref, o_ref, s_ref, y_ref = refs
    n_blocks = q_ref.shape[0] // rows
    lg = lg_ref[pl.program_id(1)]
    row = lax.broadcasted_iota(jnp.int32, (rows, rows), 0)
    col = lax.broadcasted_iota(jnp.int32, (rows, rows), 1)
    dist = jnp.abs(row - col).astype(F32)
    decay = jnp.where((col // chunk) <= (row // chunk), jnp.exp(lg * dist), 0.0)
    idx = lax.broadcasted_iota(jnp.int32, (rows, 1), 0).astype(F32)
    q_decay = jnp.exp(lg * (idx + 1.0))
    k_decay = jnp.exp(lg * (rows - 1.0 - idx))
    block_decay = jnp.exp(jnp.full((1, 1), rows, F32) * lg)

    if has_init:
        s_ref[...] = s0_ref[...]
    else:
        s_ref[...] = jnp.zeros(s_ref.shape, F32)

    def mix_block(r):
        sl = slice(r * rows, (r + 1) * rows)
        q, k, v = q_ref[sl, :], k_ref[sl, :], v_ref[sl, :]
        state = s_ref[...]
        scores = _dot_nt(q, k) * decay
        y_ref[r % 2] = _dot(scores.astype(BF16), v) + q_decay * _dot(q, state.astype(BF16))
        kd = (k.astype(F32) * k_decay).astype(BF16)
        s_ref[...] = block_decay * state + _dot_tn(kd, v)

    def gate_block(r):
        sl = slice(r * rows, (r + 1) * rows)
        y = y_ref[r % 2]
        yn = y * _rms_scale(y, NORM_EPS)
        o_ref[sl, :] = (yn * _silu(g_ref[sl, :].astype(F32))).astype(o_ref.dtype)

    for r in range(n_blocks + 1):
        if r < n_blocks:
            mix_block(r)
        if r >= 1:
            gate_block(r - 1)


def _retention(name, log_gamma, q, k, v, g, batch, rows, chunk, init_state=None):
    t = q.shape[0]
    seq = t // batch
    dk = q.shape[1] // B_HEADS
    dv = v.shape[1] // B_HEADS
    in_specs = [pl.BlockSpec(memory_space=pltpu.SMEM),
                pl.BlockSpec((seq, dk), lambda b, h: (b, h)),
                pl.BlockSpec((seq, dk), lambda b, h: (b, h)),
                pl.BlockSpec((seq, dv), lambda b, h: (b, h)),
                pl.BlockSpec((seq, dv), lambda b, h: (b, h))]
    args = [log_gamma, q, k, v, g]
    state_spec = lambda: pl.BlockSpec((None, None, dk, dv), lambda b, h: (b, h, 0, 0))
    if init_state is not None:
        in_specs.append(state_spec())
        args.append(init_state)
    return pl.pallas_call(
        functools.partial(_retention_kernel, rows=rows, chunk=chunk,
                          has_init=init_state is not None),
        grid=(batch, B_HEADS),
        in_specs=in_specs,
        out_specs=[pl.BlockSpec((seq, dv), lambda b, h: (b, h)), state_spec()],
        out_shape=[jax.ShapeDtypeStruct((t, v.shape[1]), BF16),
                   jax.ShapeDtypeStruct((batch, B_HEADS, dk, dv), F32)],
        scratch_shapes=[pltpu.VMEM((2, rows, dv), F32)],
        compiler_params=_compiler_params(2),
        name=name,
    )(*args)


def _rope_tables_a(pos):
    inv = ROPE_THETA ** (-jnp.arange(0, A_ROT_DIM, 2, dtype=F32) / A_ROT_DIM)
    ang = pos[:, None] * inv[None, :]
    cos, sin = jnp.cos(ang), jnp.sin(ang)
    d = jnp.arange(LANES) % A_HEAD_DIM
    first = d < A_ROT_HALF
    second = (d >= A_ROT_HALF) & (d < A_ROT_DIM)
    idx = d % A_ROT_HALF
    cos_t = jnp.where((first | second)[None, :], cos[:, idx], 1.0)
    from_left = jnp.where(second[None, :], sin[:, idx], 0.0)
    from_right = jnp.where(first[None, :], -sin[:, idx], 0.0)
    return cos_t, from_left, from_right


def _xpos_tables_b(pos, dk):
    inv = 1.0 / (XPOS_BASE ** jnp.linspace(0.0, 1.0, dk // 2, dtype=F32))
    ang = pos[:, None] * inv[None, :]
    return jnp.cos(ang), jnp.sin(ang)


def _row_tile(t, cap):
    return cap if t % cap == 0 else t


def kernel(x_prompt, x_sample, cache_k_a, cache_v_a, state_ret, norm_a, w_in_a, lambda_q1,
           lambda_k1, lambda_q2, lambda_k2, subln_a, w_out_a, norm_b, w_in_b, w_out_b,
           norm_final):
    batch, seq, d = x_prompt.shape
    dec_batch, dec_seq, _ = x_sample.shape
    past = cache_k_a.shape[2]
    a_heads = d // (2 * A_HEAD_DIM)
    dk = d // B_HEADS
    dv = 2 * d // B_HEADS
    lambda_init = 0.8 - 0.6 * math.exp(-0.3 * 0)

    xp = x_prompt.reshape(batch * seq, d)
    xs = x_sample.reshape(dec_batch * dec_seq, d)

    w_in_a_b = w_in_a[0].astype(BF16)
    w_out_a_b = w_out_a[0].astype(BF16)
    w_in_b_b = w_in_b[0].astype(BF16)
    w_out_b_b = w_out_b[0].astype(BF16)
    lam_params = jnp.stack([lambda_q1[0], lambda_k1[0], lambda_q2[0], lambda_k2[0]])

    pos_p = jnp.arange(seq, dtype=F32)
    pos_s = jnp.tile(past + jnp.arange(dec_seq, dtype=F32), dec_batch)
    log_gamma = jnp.log1p(-jnp.exp2(-5.0 - jnp.arange(B_HEADS, dtype=F32)))

    tm_p = _row_tile(seq, ROW_TILE)
    tm_s = dec_batch * dec_seq
    tn = PROJ_COL_TILE

    a_widths = [d, d, d, d]
    a_dtypes = [BF16, F32, F32, BF16]
    a_kernel = functools.partial(_inproj_a_kernel, q_scale=A_HEAD_DIM ** -0.5 * LOG2_E)
    qp, kp, vp, gp, khp = _inproj("inproj_a_prompt", a_kernel, xp, norm_a[0],
                                  _rope_tables_a(pos_p), w_in_a_b, a_widths, a_dtypes, tm_p, tn,
                                  k_by_head=True)
    qs, ks, vs, gs, khs = _inproj("inproj_a_sample", a_kernel, xs, norm_a[0],
                                  _rope_tables_a(pos_s), w_in_a_b, a_widths, a_dtypes, tm_s, tn,
                                  k_by_head=True)
    op = _attn_prompt(qp, kp, vp, gp, lam_params, subln_a[0], batch, seq, lambda_init,
                      tq=min(ATTN_TILE, seq))
    os_ = _attn_sample(qs, ks, vs, gs,
                       cache_k_a[0].reshape(dec_batch * past, d),
                       cache_v_a[0].reshape(dec_batch * past, d),
                       lam_params, subln_a[0], dec_batch, lambda_init, heads_per_step=4)
    xp1 = _outproj("outproj_a_prompt", op, w_out_a_b, xp, tm_p)
    xs1 = _outproj("outproj_a_sample", os_, w_out_a_b, xs, tm_s)

    b_widths = [d, d, 2 * d, 2 * d]
    b_dtypes = [BF16, BF16, BF16, BF16]
    b_kernel = functools.partial(_inproj_b_kernel, k_scale=dk ** -0.5)
    qp2, kp2, vp2, gp2 = _inproj("inproj_b_prompt", b_kernel, xp1, norm_b[0],
                                 _xpos_tables_b(pos_p, dk), w_in_b_b, b_widths, b_dtypes, tm_p, tn)
    qs2, ks2, vs2, gs2 = _inproj("inproj_b_sample", b_kernel, xs1, norm_b[0],
                                 _xpos_tables_b(pos_s, dk), w_in_b_b, b_widths, b_dtypes, tm_s, tn)
    rp, state_p = _retention("retention_prompt", log_gamma, qp2, kp2, vp2, gp2, batch,
                             rows=min(RETENTION_ROWS, seq), chunk=CHUNK)
    rs, state_s = _retention("retention_sample", log_gamma, qs2, ks2, vs2, gs2, dec_batch,
                             rows=dec_seq, chunk=dec_seq, init_state=state_ret[0])
    yp = _outproj("outproj_b_prompt", rp, w_out_b_b, xp1, tm_p, final_norm_w=norm_final)
    ys = _outproj("outproj_b_sample", rs, w_out_b_b, xs1, tm_s, final_norm_w=norm_final)

    return (yp.reshape(batch, seq, d),
            ys.reshape(dec_batch, dec_seq, d),
            khp.reshape(1, batch, seq, 2 * a_heads, A_HEAD_DIM),
            vp.reshape(1, batch, seq, a_heads, 2 * A_HEAD_DIM),
            state_p[None],
            khs.reshape(1, dec_batch, dec_seq, 2 * a_heads, A_HEAD_DIM),
            vs.reshape(1, dec_batch, dec_seq, a_heads, 2 * A_HEAD_DIM),
            state_s[None])
```

```python
import functools
import math

import jax
import jax.numpy as jnp
from jax import lax
from jax.experimental import pallas as pl
from jax.experimental.pallas import tpu as pltpu

F32 = jnp.float32
BF16 = jnp.bfloat16

CHUNK = 64
A_HEAD_DIM = 64
A_ROT_DIM = A_HEAD_DIM // 4
A_ROT_HALF = A_ROT_DIM // 2
ROPE_THETA = 500000.0
B_HEADS = 8
XPOS_BASE = 10000.0
NORM_EPS = 1e-6
SUBLN_EPS = 1e-5

LANES = 128
SUBLANES = 8
MXU_DIM = 256
SUM_ROWS = 16
LOG2_E = math.log2(math.e)
VMEM_LIMIT_BYTES = 56 * 1024 * 1024

ROW_TILE = 512
PROJ_COL_TILE = 512
ATTN_TILE = 256
RETENTION_ROWS = 256


def _compiler_params(n_axes):
    return pltpu.CompilerParams(
        dimension_semantics=("arbitrary",) * n_axes,
        vmem_limit_bytes=VMEM_LIMIT_BYTES)


def _rms_scale(x, eps):
    return lax.rsqrt(jnp.mean(x * x, axis=-1, keepdims=True) + eps)


def _silu(g):
    h = 0.5 * g
    return h + h * jnp.tanh(h)


def _dot(a, b):
    return jnp.dot(a, b, preferred_element_type=F32)


def _dot_nt(a, b):
    return lax.dot_general(a, b, (((1,), (1,)), ((), ())), preferred_element_type=F32)


def _dot_tn(a, b):
    return lax.dot_general(a, b, (((0,), (0,)), ((), ())), preferred_element_type=F32)


def _normalize_rows(x_ref, nw_ref, xn_ref):
    @pl.when(pl.program_id(1) == 0)
    def _():
        x = x_ref[...]
        xn_ref[...] = (x * _rms_scale(x, NORM_EPS) * nw_ref[...]).astype(BF16)


def _inproj_a_kernel(x_ref, nw_ref, cos_ref, sa_ref, sb_ref, wq_ref, wk_ref, wv_ref, wg_ref,
                     q_ref, k_ref, v_ref, g_ref, kh_ref, vh_ref, xn_ref, *, q_scale):
    _normalize_rows(x_ref, nw_ref, xn_ref)
    xn = xn_ref[...]
    cos, sa, sb = cos_ref[...], sa_ref[...], sb_ref[...]

    def rope_store(z, out_ref, scale, heads_ref=None):
        rotated = []
        for c in range(z.shape[1] // LANES):
            zc = z[:, c * LANES:(c + 1) * LANES]
            r = (zc * cos + pltpu.roll(zc, A_ROT_HALF, 1) * sa
                 + pltpu.roll(zc, LANES - A_ROT_HALF, 1) * sb)
            if scale != 1.0:
                r = r * scale
            out_ref[:, c * LANES:(c + 1) * LANES] = r.astype(out_ref.dtype)
            rotated.append(r)
        if heads_ref is not None:
            heads_ref[...] = pltpu.einshape("m(ud)->mud", jnp.concatenate(rotated, axis=1),
                                            u=SUBLANES)

    rope_store(_dot(xn, wq_ref[...]), q_ref, q_scale)
    rope_store(_dot(xn, wk_ref[...]), k_ref, 1.0, kh_ref)
    g_ref[...] = _dot(xn, wg_ref[...]).astype(g_ref.dtype)

    @pl.when(pl.program_id(1) % 2 == 0)
    def _():
        zv = _dot(xn, wv_ref[...])
        v_ref[...] = zv
        vh_ref[...] = pltpu.einshape("m(hd)->mhd", zv, h=SUBLANES)


def _inproj_b_kernel(x_ref, nw_ref, cos_ref, sin_ref, wq_ref, wk_ref, wv_ref, wg_ref,
                     q_ref, k_ref, v_ref, g_ref, xn_ref, *, k_scale):
    _normalize_rows(x_ref, nw_ref, xn_ref)
    xn = xn_ref[...]
    cos, sin = cos_ref[...], sin_ref[...]

    def rope_store(z, out_ref, scale):
        for hh in range(z.shape[1] // (2 * LANES)):
            c0 = hh * 2 * LANES
            lo = z[:, c0:c0 + LANES]
            hi = z[:, c0 + LANES:c0 + 2 * LANES]
            r_lo = lo * cos - hi * sin
            r_hi = lo * sin + hi * cos
            if scale != 1.0:
                r_lo, r_hi = r_lo * scale, r_hi * scale
            out_ref[:, c0:c0 + LANES] = r_lo.astype(out_ref.dtype)
            out_ref[:, c0 + LANES:c0 + 2 * LANES] = r_hi.astype(out_ref.dtype)

    rope_store(_dot(xn, wq_ref[...]), q_ref, 1.0)
    rope_store(_dot(xn, wk_ref[...]), k_ref, k_scale)
    v_ref[...] = _dot(xn, wv_ref[...]).astype(v_ref.dtype)
    g_ref[...] = _dot(xn, wg_ref[...]).astype(g_ref.dtype)


def _inproj(name, kernel_fn, x, norm_w, tables, w_bf16, widths, out_dtypes, tm, tn_unit,
            every=None, by_head=()):
    t, d = x.shape
    every = every or [1] * len(widths)
    nj = min(widths) // tn_unit
    tns = [e * (w // nj) for w, e in zip(widths, every)]
    offs = [sum(widths[:s]) // tns[s] for s in range(len(widths))]
    n_pos_blocks = tables[0].shape[0] // tm

    def col_block(s):
        return functools.partial(lambda i, j, off, e: (0, off + j // e), off=offs[s], e=every[s])

    def out_block(s, rank):
        return functools.partial(lambda i, j, e: (i, j // e) + (0,) * (rank - 2), e=every[s])

    in_specs = [pl.BlockSpec((tm, d), lambda i, j: (i, 0)),
                pl.BlockSpec((1, d), lambda i, j: (0, 0))]
    in_specs += [pl.BlockSpec((tm, LANES), lambda i, j: (i % n_pos_blocks, 0)) for _ in tables]
    in_specs += [pl.BlockSpec((d, tns[s]), col_block(s)) for s in range(len(widths))]
    out_specs = [pl.BlockSpec((tm, tns[s]), out_block(s, 2)) for s in range(len(widths))]
    out_shape = [jax.ShapeDtypeStruct((t, widths[s]), out_dtypes[s]) for s in range(len(widths))]
    for s, head_width in by_head:
        assert tns[s] == SUBLANES * head_width and nj % every[s] == 0
        out_specs.append(pl.BlockSpec((tm, SUBLANES, head_width), out_block(s, 3)))
        out_shape.append(jax.ShapeDtypeStruct((t, widths[s] // head_width, head_width), F32))
    return pl.pallas_call(
        kernel_fn,
        grid=(t // tm, nj),
        in_specs=in_specs,
        out_specs=out_specs,
        out_shape=out_shape,
        scratch_shapes=[pltpu.VMEM((tm, d), BF16)],
        compiler_params=_compiler_params(2),
        name=name,
    )(x, norm_w.reshape(1, d), *tables, *([w_bf16] * len(widths)))


def _lambda_value(lam_ref, lambda_init):
    lp = lam_ref[...]
    s1 = jnp.sum(lp[0:1] * lp[1:2], axis=-1, keepdims=True)
    s2 = jnp.sum(lp[2:3] * lp[3:4], axis=-1, keepdims=True)
    return jnp.exp(s1) - jnp.exp(s2) + lambda_init


def _split_subheads(q):
    lane = lax.broadcasted_iota(jnp.int32, q.shape, 1)
    qf = q.astype(F32)
    return jnp.where(lane < A_HEAD_DIM, qf, 0.0), jnp.where(lane >= A_HEAD_DIM, qf, 0.0)


def _diff_finish(o1, o2, lam, sg, g, lambda_init):
    o = o1 - lam * o2
    on = o * _rms_scale(o, SUBLN_EPS) * sg
    on = on * (1.0 - lambda_init)
    return on * _silu(g)


def _attn_prompt_kernel(q_ref, k_ref, v_ref, g_ref, lam_ref, sg_ref, o_ref,
                        kb_ref, vt_ref, qt_ref, m_ref, acc_ref, s_ref, p_ref, a_ref,
                        *, tq, lambda_init):
    seq = q_ref.shape[0]
    n_tiles = seq // tq
    lam = _lambda_value(lam_ref, lambda_init)
    sg = sg_ref[...]
    ones_row = lax.broadcasted_iota(jnp.int32, (SUM_ROWS, tq), 0) == 0
    key = lax.broadcasted_iota(jnp.int32, (tq, tq), 0)
    qry = lax.broadcasted_iota(jnp.int32, (tq, tq), 1)
    diag_mask = (key // CHUNK) <= (qry // CHUNK)

    def prepare_tile(t):
        rows = slice(t * tq, (t + 1) * tq)
        kb_ref[rows, :] = k_ref[rows, :].astype(BF16)
        vt_ref[t, 0:LANES] = v_ref[rows, :].T.astype(BF16)
        vt_ref[t, LANES:LANES + SUM_ROWS] = jnp.where(ones_row, 1.0, 0.0).astype(BF16)
        qa, qb = _split_subheads(q_ref[rows, :])
        qt_ref[t, 0] = qa.T.astype(BF16)
        qt_ref[t, 1] = qb.T.astype(BF16)
        m_ref[t] = jnp.full(m_ref.shape[1:], -jnp.inf, F32)
        acc_ref[t] = jnp.zeros(acc_ref.shape[1:], F32)

    def finish_tile(qi):
        rows = slice(qi * tq, (qi + 1) * tq)
        o1, o2 = [(acc_ref[qi, i, 0:LANES] * (1.0 / acc_ref[qi, i, LANES:LANES + 1])).T
                  for i in range(2)]
        out = _diff_finish(o1, o2, lam, sg, g_ref[rows, :].astype(F32), lambda_init)
        o_ref[rows, :] = out.astype(o_ref.dtype)

    items = [(qi, kt) for qi in range(n_tiles) for kt in range(qi + 1)]

    def score_stage(j):
        qi, kt = items[j]
        if kt == 0:
            prepare_tile(qi)
        for i in range(2):
            s_ref[j % 2, i] = _dot(kb_ref[kt * tq:(kt + 1) * tq, :], qt_ref[qi, i])

    def softmax_stage(j):
        qi, kt = items[j]
        for i in range(2):
            s = s_ref[j % 2, i]
            if kt == qi:
                s = jnp.where(diag_mask, s, -jnp.inf)
            m_old = m_ref[qi, i]
            m_new = jnp.maximum(m_old, jnp.max(s, axis=0, keepdims=True))
            m_ref[qi, i] = m_new
            a_ref[j % 2, i] = jnp.exp2(m_old - m_new)
            p_ref[j % 2, i] = jnp.exp2(s - m_new).astype(BF16)

    def value_stage(j):
        qi, kt = items[j]
        for i in range(2):
            acc_ref[qi, i] = (a_ref[j % 2, i] * acc_ref[qi, i]
                              + _dot(vt_ref[kt], p_ref[j % 2, i]))
        if kt == qi:
            finish_tile(qi)

    for step in range(len(items) + 2):
        for stage, lag in ((value_stage, 2), (softmax_stage, 1), (score_stage, 0)):
            if 0 <= step - lag < len(items):
                stage(step - lag)


def _attn_prompt(q, k, v, g, lam_params, subln, batch, seq, lambda_init, tq):
    t, width = q.shape
    heads = width // LANES
    n_tiles = seq // tq
    blk = lambda: pl.BlockSpec((seq, LANES), lambda b, h: (b, h))
    return pl.pallas_call(
        functools.partial(_attn_prompt_kernel, tq=tq, lambda_init=lambda_init),
        grid=(batch, heads),
        in_specs=[blk(), blk(), blk(), blk(),
                  pl.BlockSpec(lam_params.shape, lambda b, h: (0, 0)),
                  pl.BlockSpec((1, LANES), lambda b, h: (0, 0))],
        out_specs=blk(),
        out_shape=jax.ShapeDtypeStruct((t, width), BF16),
        scratch_shapes=[pltpu.VMEM((seq, LANES), BF16),
                        pltpu.VMEM((n_tiles, LANES + SUM_ROWS, tq), BF16),
                        pltpu.VMEM((n_tiles, 2, LANES, tq), BF16),
                        pltpu.VMEM((n_tiles, 2, 1, tq), F32),
                        pltpu.VMEM((n_tiles, 2, LANES + SUM_ROWS, tq), F32),
                        pltpu.VMEM((2, 2, tq, tq), F32),
                        pltpu.VMEM((2, 2, tq, tq), BF16),
                        pltpu.VMEM((2, 2, 1, tq), F32)],
        compiler_params=_compiler_params(2),
        name="attn_prompt",
    )(q, k, v, g, lam_params, subln.reshape(1, LANES))


def _attn_sample_kernel(q_ref, kn_ref, vn_ref, ck_ref, cv_ref, g_ref, lam_ref, sg_ref, o_ref,
                        *, heads_per_step, lambda_init):
    lam = _lambda_value(lam_ref, lambda_init)
    sg = sg_ref[...]
    for hh in range(heads_per_step):
        cs = slice(hh * LANES, (hh + 1) * LANES)
        qs = jnp.concatenate(_split_subheads(q_ref[:, cs]), axis=0).astype(BF16)
        kc = ck_ref[:, cs].astype(BF16)
        vc = cv_ref[:, cs].astype(BF16)
        kn = kn_ref[:, cs].astype(BF16)
        vn = vn_ref[:, cs].astype(BF16)
        sc = _dot_nt(qs, kc)
        sn = _dot_nt(qs, kn)
        m = jnp.maximum(jnp.max(sc, axis=-1, keepdims=True),
                        jnp.max(sn, axis=-1, keepdims=True))
        pc = jnp.exp2(sc - m)
        pn = jnp.exp2(sn - m)
        l = jnp.sum(pc, axis=-1, keepdims=True) + jnp.sum(pn, axis=-1, keepdims=True)
        acc = _dot(pc.astype(BF16), vc) + _dot(pn.astype(BF16), vn)
        o12 = acc * (1.0 / l)
        frames = q_ref.shape[0]
        out = _diff_finish(o12[:frames], o12[frames:], lam, sg, g_ref[:, cs].astype(F32),
                           lambda_init)
        o_ref[:, cs] = out.astype(o_ref.dtype)


def _attn_sample(q, k, v, g, cache_k, cache_v, lam_params, subln, batch, lambda_init,
                 heads_per_step):
    t, width = q.shape
    frames = t // batch
    past = cache_k.shape[0] // batch
    cols = heads_per_step * LANES
    new = lambda: pl.BlockSpec((frames, cols), lambda b, h: (b, h))
    old = lambda: pl.BlockSpec((past, cols), lambda b, h: (b, h))
    return pl.pallas_call(
        functools.partial(_attn_sample_kernel, heads_per_step=heads_per_step,
                          lambda_init=lambda_init),
        grid=(batch, width // cols),
        in_specs=[new(), new(), new(), old(), old(), new(),
                  pl.BlockSpec(lam_params.shape, lambda b, h: (0, 0)),
                  pl.BlockSpec((1, LANES), lambda b, h: (0, 0))],
        out_specs=new(),
        out_shape=jax.ShapeDtypeStruct((t, width), BF16),
        compiler_params=_compiler_params(2),
        name="attn_sample",
    )(q, k, v, cache_k, cache_v, g, lam_params, subln.reshape(1, LANES))


def _outproj_kernel(a_ref, w_ref, x_ref, y_ref):
    y_ref[...] = x_ref[...] + _dot(a_ref[...], w_ref[...])


def _outproj_final_kernel(a_ref, w_ref, x_ref, nf_ref, y_ref):
    xo = x_ref[...] + _dot(a_ref[...], w_ref[...])
    y_ref[...] = xo * _rms_scale(xo, NORM_EPS) * nf_ref[...]


def _outproj(name, a, w_bf16, x, tm, final_norm_w=None):
    t, kdim = a.shape
    d = x.shape[1]
    in_specs = [pl.BlockSpec((tm, kdim), lambda i: (i, 0)),
                pl.BlockSpec((kdim, d), lambda i: (0, 0), pipeline_mode=pl.Buffered(1)),
                pl.BlockSpec((tm, d), lambda i: (i, 0))]
    args = [a, w_bf16, x]
    kernel_fn = _outproj_kernel
    if final_norm_w is not None:
        in_specs.append(pl.BlockSpec((1, d), lambda i: (0, 0)))
        args.append(final_norm_w.reshape(1, d))
        kernel_fn = _outproj_final_kernel
    return pl.pallas_call(
        kernel_fn,
        grid=(t // tm,),
        in_specs=in_specs,
        out_specs=pl.BlockSpec((tm, d), lambda i: (i, 0)),
        out_shape=jax.ShapeDtypeStruct((t, d), F32),
        compiler_params=_compiler_params(1),
        name=name,
    )(*args)


def _retention_kernel(*refs, rows, chunk, has_init):
    if has_init:
        lg_ref, q_ref, k_ref, v_ref, g_ref, s0_ref, o_ref, s_ref, y_ref = refs
    else:
        lg_ref, q_ref, k_ref, v_ref, g_ref, o_ref, s_ref, y_ref = refs
    n_blocks = q_ref.shape[0] // rows
    lg = lg_ref[pl.program_id(1)]
    row = lax.broadcasted_iota(jnp.int32, (rows, rows), 0)
    col = lax.broadcasted_iota(jnp.int32, (rows, rows), 1)
    dist = jnp.abs(row - col).astype(F32)
    decay = jnp.where((col // chunk) <= (row // chunk), jnp.exp(lg * dist), 0.0)
    idx = lax.broadcasted_iota(jnp.int32, (rows, 1), 0).astype(F32)
    q_decay = jnp.exp(lg * (idx + 1.0))
    k_decay = jnp.exp(lg * (rows - 1.0 - idx))
    block_decay = jnp.exp(jnp.full((1, 1), rows, F32) * lg)

    if has_init:
        s_ref[...] = s0_ref[...]
    else:
        s_ref[...] = jnp.zeros(s_ref.shape, F32)

    def mix_block(r):
        sl = slice(r * rows, (r + 1) * rows)
        q, k, v = q_ref[sl, :], k_ref[sl, :], v_ref[sl, :]
        state = s_ref[...]
        scores = _dot_nt(q, k) * decay
        y_ref[r % 2] = _dot(scores.astype(BF16), v) + q_decay * _dot(q, state.astype(BF16))
        kd = (k.astype(F32) * k_decay).astype(BF16)
        s_ref[...] = block_decay * state + _dot_tn(kd, v)

    def gate_block(r):
        sl = slice(r * rows, (r + 1) * rows)
        y = y_ref[r % 2]
        yn = y * _rms_scale(y, NORM_EPS)
        o_ref[sl, :] = (yn * _silu(g_ref[sl, :].astype(F32))).astype(o_ref.dtype)

    for r in range(n_blocks + 1):
        if r < n_blocks:
            mix_block(r)
        if r >= 1:
            gate_block(r - 1)


def _retention(name, log_gamma, q, k, v, g, batch, rows, chunk, init_state=None):
    t = q.shape[0]
    seq = t // batch
    dk = q.shape[1] // B_HEADS
    dv = v.shape[1] // B_HEADS
    in_specs = [pl.BlockSpec(memory_space=pltpu.SMEM),
                pl.BlockSpec((seq, dk), lambda b, h: (b, h)),
                pl.BlockSpec((seq, dk), lambda b, h: (b, h)),
                pl.BlockSpec((seq, dv), lambda b, h: (b, h)),
                pl.BlockSpec((seq, dv), lambda b, h: (b, h))]
    args = [log_gamma, q, k, v, g]
    state_spec = lambda: pl.BlockSpec((None, None, dk, dv), lambda b, h: (b, h, 0, 0))
    if init_state is not None:
        in_specs.append(state_spec())
        args.append(init_state)
    return pl.pallas_call(
        functools.partial(_retention_kernel, rows=rows, chunk=chunk,
                          has_init=init_state is not None),
        grid=(batch, B_HEADS),
        in_specs=in_specs,
        out_specs=[pl.BlockSpec((seq, dv), lambda b, h: (b, h)), state_spec()],
        out_shape=[jax.ShapeDtypeStruct((t, v.shape[1]), BF16),
                   jax.ShapeDtypeStruct((batch, B_HEADS, dk, dv), F32)],
        scratch_shapes=[pltpu.VMEM((2, rows, dv), F32)],
        compiler_params=_compiler_params(2),
        name=name,
    )(*args)


def _rope_tables_a(pos):
    inv = ROPE_THETA ** (-jnp.arange(0, A_ROT_DIM, 2, dtype=F32) / A_ROT_DIM)
    ang = pos[:, None] * inv[None, :]
    cos, sin = jnp.cos(ang), jnp.sin(ang)
    d = jnp.arange(LANES) % A_HEAD_DIM
    first = d < A_ROT_HALF
    second = (d >= A_ROT_HALF) & (d < A_ROT_DIM)
    idx = d % A_ROT_HALF
    cos_t = jnp.where((first | second)[None, :], cos[:, idx], 1.0)
    from_left = jnp.where(second[None, :], sin[:, idx], 0.0)
    from_right = jnp.where(first[None, :], -sin[:, idx], 0.0)
    return cos_t, from_left, from_right


def _xpos_tables_b(pos, dk):
    inv = 1.0 / (XPOS_BASE ** jnp.linspace(0.0, 1.0, dk // 2, dtype=F32))
    ang = pos[:, None] * inv[None, :]
    return jnp.cos(ang), jnp.sin(ang)


def _row_tile(t, cap):
    return cap if t % cap == 0 else t


def kernel(x_prompt, x_sample, cache_k_a, cache_v_a, state_ret, norm_a, w_in_a, lambda_q1,
           lambda_k1, lambda_q2, lambda_k2, subln_a, w_out_a, norm_b, w_in_b, w_out_b,
           norm_final):
    batch, seq, d = x_prompt.shape
    dec_batch, dec_seq, _ = x_sample.shape
    past = cache_k_a.shape[2]
    a_heads = d // (2 * A_HEAD_DIM)
    dk = d // B_HEADS
    dv = 2 * d // B_HEADS
    lambda_init = 0.8 - 0.6 * math.exp(-0.3 * 0)

    xp = x_prompt.reshape(batch * seq, d)
    xs = x_sample.reshape(dec_batch * dec_seq, d)

    w_in_a_b = w_in_a[0].astype(BF16)
    w_out_a_b = w_out_a[0].astype(BF16)
    w_in_b_b = w_in_b[0].astype(BF16)
    w_out_b_b = w_out_b[0].astype(BF16)
    lam_params = jnp.stack([lambda_q1[0], lambda_k1[0], lambda_q2[0], lambda_k2[0]])

    pos_p = jnp.arange(seq, dtype=F32)
    pos_s = jnp.tile(past + jnp.arange(dec_seq, dtype=F32), dec_batch)
    log_gamma = jnp.log1p(-jnp.exp2(-5.0 - jnp.arange(B_HEADS, dtype=F32)))

    tm_p = _row_tile(seq, ROW_TILE)
    tm_s = dec_batch * dec_seq
    tn = PROJ_COL_TILE

    a_widths = [d, d, d, d]
    a_dtypes = [BF16, F32, F32, BF16]
    a_kernel = functools.partial(_inproj_a_kernel, q_scale=A_HEAD_DIM ** -0.5 * LOG2_E)
    a_layout = dict(every=[1, 1, 2, 1], by_head=[(1, A_HEAD_DIM), (2, 2 * A_HEAD_DIM)])
    qp, kp, vp, gp, khp, vhp = _inproj("inproj_a_prompt", a_kernel, xp, norm_a[0],
                                       _rope_tables_a(pos_p), w_in_a_b, a_widths, a_dtypes,
                                       tm_p, tn, **a_layout)
    qs, ks, vs, gs, khs, vhs = _inproj("inproj_a_sample", a_kernel, xs, norm_a[0],
                                       _rope_tables_a(pos_s), w_in_a_b, a_widths, a_dtypes,
                                       tm_s, tn, **a_layout)
    op = _attn_prompt(qp, kp, vp, gp, lam_params, subln_a[0], batch, seq, lambda_init,
                      tq=min(ATTN_TILE, seq))
    os_ = _attn_sample(qs, ks, vs, gs,
                       cache_k_a[0].reshape(dec_batch * past, d),
                       cache_v_a[0].reshape(dec_batch * past, d),
                       lam_params, subln_a[0], dec_batch, lambda_init, heads_per_step=4)
    xp1 = _outproj("outproj_a_prompt", op, w_out_a_b, xp, tm_p)
    xs1 = _outproj("outproj_a_sample", os_, w_out_a_b, xs, tm_s)

    b_widths = [d, d, 2 * d, 2 * d]
    b_dtypes = [BF16, BF16, BF16, BF16]
    b_kernel = functools.partial(_inproj_b_kernel, k_scale=dk ** -0.5)
    qp2, kp2, vp2, gp2 = _inproj("inproj_b_prompt", b_kernel, xp1, norm_b[0],
                                 _xpos_tables_b(pos_p, dk), w_in_b_b, b_widths, b_dtypes, tm_p, tn)
    qs2, ks2, vs2, gs2 = _inproj("inproj_b_sample", b_kernel, xs1, norm_b[0],
                                 _xpos_tables_b(pos_s, dk), w_in_b_b, b_widths, b_dtypes, tm_s, tn)
    rp, state_p = _retention("retention_prompt", log_gamma, qp2, kp2, vp2, gp2, batch,
                             rows=min(RETENTION_ROWS, seq), chunk=CHUNK)
    rs, state_s = _retention("retention_sample", log_gamma, qs2, ks2, vs2, gs2, dec_batch,
                             rows=dec_seq, chunk=dec_seq, init_state=state_ret[0])
    yp = _outproj("outproj_b_prompt", rp, w_out_b_b, xp1, tm_p, final_norm_w=norm_final)
    ys = _outproj("outproj_b_sample", rs, w_out_b_b, xs1, tm_s, final_norm_w=norm_final)

    return (yp.reshape(batch, seq, d),
            ys.reshape(dec_batch, dec_seq, d),
            khp.reshape(1, batch, seq, 2 * a_heads, A_HEAD_DIM),
            vhp.reshape(1, batch, seq, a_heads, 2 * A_HEAD_DIM),
            state_p[None],
            khs.reshape(1, dec_batch, dec_seq, 2 * a_heads, A_HEAD_DIM),
            vhs.reshape(1, dec_batch, dec_seq, a_heads, 2 * A_HEAD_DIM),
            state_s[None])
```

```python
import functools
import math

import jax
import jax.numpy as jnp
from jax import lax
from jax.experimental import pallas as pl
from jax.experimental.pallas import tpu as pltpu

F32 = jnp.float32
BF16 = jnp.bfloat16

CHUNK = 64
A_HEAD_DIM = 64
A_ROT_DIM = A_HEAD_DIM // 4
A_ROT_HALF = A_ROT_DIM // 2
ROPE_THETA = 500000.0
B_HEADS = 8
XPOS_BASE = 10000.0
NORM_EPS = 1e-6
SUBLN_EPS = 1e-5

LANES = 128
SUBLANES = 8
MXU_DIM = 256
SUM_ROWS = 16
LOG2_E = math.log2(math.e)
VMEM_LIMIT_BYTES = 56 * 1024 * 1024

ROW_TILE = 512
PROJ_COL_TILE = 512
ATTN_TILE = 256
RETENTION_ROWS = 256


def _compiler_params(n_axes):
    return pltpu.CompilerParams(
        dimension_semantics=("arbitrary",) * n_axes,
        vmem_limit_bytes=VMEM_LIMIT_BYTES)


def _rms_scale(x, eps):
    return lax.rsqrt(jnp.mean(x * x, axis=-1, keepdims=True) + eps)


def _silu(g):
    h = 0.5 * g
    return h + h * jnp.tanh(h)


def _dot(a, b):
    return jnp.dot(a, b, preferred_element_type=F32)


def _dot_nt(a, b):
    return lax.dot_general(a, b, (((1,), (1,)), ((), ())), preferred_element_type=F32)


def _dot_tn(a, b):
    return lax.dot_general(a, b, (((0,), (0,)), ((), ())), preferred_element_type=F32)


def _normalize_rows(x_ref, nw_ref, xn_ref):
    @pl.when(pl.program_id(1) == 0)
    def _():
        x = x_ref[...]
        xn_ref[...] = (x * _rms_scale(x, NORM_EPS) * nw_ref[...]).astype(BF16)


def _inproj_a_kernel(x_ref, nw_ref, cos_ref, sa_ref, sb_ref, wq_ref, wk_ref, wv_ref, wg_ref,
                     q_ref, k_ref, v_ref, g_ref, kh_ref, xn_ref, *, q_scale):
    _normalize_rows(x_ref, nw_ref, xn_ref)
    xn = xn_ref[...]
    cos, sa, sb = cos_ref[...], sa_ref[...], sb_ref[...]

    def rope_store(z, out_ref, scale, heads_ref=None):
        rotated = []
        for c in range(z.shape[1] // LANES):
            zc = z[:, c * LANES:(c + 1) * LANES]
            r = (zc * cos + pltpu.roll(zc, A_ROT_HALF, 1) * sa
                 + pltpu.roll(zc, LANES - A_ROT_HALF, 1) * sb)
            if scale != 1.0:
                r = r * scale
            out_ref[:, c * LANES:(c + 1) * LANES] = r.astype(out_ref.dtype)
            rotated.append(r)
        if heads_ref is not None:
            heads_ref[...] = pltpu.einshape("m(ud)->mud", jnp.concatenate(rotated, axis=1),
                                            u=SUBLANES)

    rope_store(_dot(xn, wq_ref[...]), q_ref, q_scale)
    rope_store(_dot(xn, wk_ref[...]), k_ref, 1.0, kh_ref)
    v_ref[...] = _dot(xn, wv_ref[...])
    g_ref[...] = _dot(xn, wg_ref[...]).astype(g_ref.dtype)


def _inproj_b_kernel(x_ref, nw_ref, cos_ref, sin_ref, wq_ref, wk_ref, wv_ref, wg_ref,
                     q_ref, k_ref, v_ref, g_ref, xn_ref, *, k_scale):
    _normalize_rows(x_ref, nw_ref, xn_ref)
    xn = xn_ref[...]
    cos, sin = cos_ref[...], sin_ref[...]

    def rope_store(z, out_ref, scale):
        for hh in range(z.shape[1] // (2 * LANES)):
            c0 = hh * 2 * LANES
            lo = z[:, c0:c0 + LANES]
            hi = z[:, c0 + LANES:c0 + 2 * LANES]
            r_lo = lo * cos - hi * sin
            r_hi = lo * sin + hi * cos
            if scale != 1.0:
                r_lo, r_hi = r_lo * scale, r_hi * scale
            out_ref[:, c0:c0 + LANES] = r_lo.astype(out_ref.dtype)
            out_ref[:, c0 + LANES:c0 + 2 * LANES] = r_hi.astype(out_ref.dtype)

    rope_store(_dot(xn, wq_ref[...]), q_ref, 1.0)
    rope_store(_dot(xn, wk_ref[...]), k_ref, k_scale)
    v_ref[...] = _dot(xn, wv_ref[...]).astype(v_ref.dtype)
    g_ref[...] = _dot(xn, wg_ref[...]).astype(g_ref.dtype)


def _inproj(name, kernel_fn, x, norm_w, tables, w_bf16, widths, out_dtypes, tm, tn_unit,
            by_head=()):
    t, d = x.shape
    nj = min(widths) // tn_unit
    tns = [w // nj for w in widths]
    offs = [sum(widths[:s]) // tns[s] for s in range(len(widths))]
    n_pos_blocks = tables[0].shape[0] // tm

    in_specs = [pl.BlockSpec((tm, d), lambda i, j: (i, 0)),
                pl.BlockSpec((1, d), lambda i, j: (0, 0))]
    in_specs += [pl.BlockSpec((tm, LANES), lambda i, j: (i % n_pos_blocks, 0)) for _ in tables]
    for s in range(len(widths)):
        in_specs.append(pl.BlockSpec((d, tns[s]), functools.partial(
            lambda i, j, off: (0, off + j), off=offs[s])))
    out_specs = [pl.BlockSpec((tm, tns[s]), lambda i, j: (i, j)) for s in range(len(widths))]
    out_shape = [jax.ShapeDtypeStruct((t, widths[s]), out_dtypes[s]) for s in range(len(widths))]
    for s, head_width in by_head:
        assert tns[s] == SUBLANES * head_width
        out_specs.append(pl.BlockSpec((tm, SUBLANES, head_width), lambda i, j: (i, j, 0)))
        out_shape.append(jax.ShapeDtypeStruct((t, widths[s] // head_width, head_width), F32))
    return pl.pallas_call(
        kernel_fn,
        grid=(t // tm, nj),
        in_specs=in_specs,
        out_specs=out_specs,
        out_shape=out_shape,
        scratch_shapes=[pltpu.VMEM((tm, d), BF16)],
        compiler_params=_compiler_params(2),
        name=name,
    )(x, norm_w.reshape(1, d), *tables, *([w_bf16] * len(widths)))


def _lambda_value(lam_ref, lambda_init):
    lp = lam_ref[...]
    s1 = jnp.sum(lp[0:1] * lp[1:2], axis=-1, keepdims=True)
    s2 = jnp.sum(lp[2:3] * lp[3:4], axis=-1, keepdims=True)
    return jnp.exp(s1) - jnp.exp(s2) + lambda_init


def _split_subheads(q):
    lane = lax.broadcasted_iota(jnp.int32, q.shape, 1)
    qf = q.astype(F32)
    return jnp.where(lane < A_HEAD_DIM, qf, 0.0), jnp.where(lane >= A_HEAD_DIM, qf, 0.0)


def _diff_finish(o1, o2, lam, sg, g, lambda_init):
    o = o1 - lam * o2
    on = o * _rms_scale(o, SUBLN_EPS) * sg
    on = on * (1.0 - lambda_init)
    return on * _silu(g)


def _attn_prompt_kernel(q_ref, k_ref, v_ref, g_ref, lam_ref, sg_ref, o_ref,
                        vt_ref, qt_ref, m_ref, acc_ref, s_ref, p_ref, a_ref,
                        *, tq, lambda_init):
    seq = q_ref.shape[0]
    n_tiles = seq // tq
    lam = _lambda_value(lam_ref, lambda_init)
    sg = sg_ref[...]
    ones_row = lax.broadcasted_iota(jnp.int32, (SUM_ROWS, tq), 0) == 0
    key = lax.broadcasted_iota(jnp.int32, (tq, tq), 0)
    qry = lax.broadcasted_iota(jnp.int32, (tq, tq), 1)
    diag_mask = (key // CHUNK) <= (qry // CHUNK)

    def prepare_tile(t):
        rows = slice(t * tq, (t + 1) * tq)
        vt_ref[t, 0:LANES] = v_ref[rows, :].T.astype(BF16)
        vt_ref[t, LANES:LANES + SUM_ROWS] = jnp.where(ones_row, 1.0, 0.0).astype(BF16)
        qa, qb = _split_subheads(q_ref[rows, :])
        qt_ref[t, 0] = qa.T.astype(BF16)
        qt_ref[t, 1] = qb.T.astype(BF16)
        m_ref[t] = jnp.full(m_ref.shape[1:], -jnp.inf, F32)
        acc_ref[t] = jnp.zeros(acc_ref.shape[1:], F32)

    def finish_tile(qi):
        rows = slice(qi * tq, (qi + 1) * tq)
        o1, o2 = [(acc_ref[qi, i, 0:LANES] * (1.0 / acc_ref[qi, i, LANES:LANES + 1])).T
                  for i in range(2)]
        out = _diff_finish(o1, o2, lam, sg, g_ref[rows, :].astype(F32), lambda_init)
        o_ref[rows, :] = out.astype(o_ref.dtype)

    items = [(qi, kt) for qi in range(n_tiles) for kt in range(qi + 1)]

    def score_stage(j):
        qi, kt = items[j]
        if kt == 0:
            prepare_tile(qi)
        for i in range(2):
            s_ref[j % 2, i] = _dot(k_ref[kt * tq:(kt + 1) * tq, :], qt_ref[qi, i])

    def softmax_stage(j):
        qi, kt = items[j]
        for i in range(2):
            s = s_ref[j % 2, i]
            if kt == qi:
                s = jnp.where(diag_mask, s, -jnp.inf)
            m_old = m_ref[qi, i]
            m_new = jnp.maximum(m_old, jnp.max(s, axis=0, keepdims=True))
            m_ref[qi, i] = m_new
            a_ref[j % 2, i] = jnp.exp2(m_old - m_new)
            p_ref[j % 2, i] = jnp.exp2(s - m_new).astype(BF16)

    def value_stage(j):
        qi, kt = items[j]
        for i in range(2):
            acc_ref[qi, i] = (a_ref[j % 2, i] * acc_ref[qi, i]
                              + _dot(vt_ref[kt], p_ref[j % 2, i]))
        if kt == qi:
            finish_tile(qi)

    for step in range(len(items) + 2):
        for stage, lag in ((value_stage, 2), (softmax_stage, 1), (score_stage, 0)):
            if 0 <= step - lag < len(items):
                stage(step - lag)


def _attn_prompt(q, k, v, g, lam_params, subln, batch, seq, lambda_init, tq):
    t, width = q.shape
    heads = width // LANES
    n_tiles = seq // tq
    blk = lambda: pl.BlockSpec((seq, LANES), lambda b, h: (b, h))
    return pl.pallas_call(
        functools.partial(_attn_prompt_kernel, tq=tq, lambda_init=lambda_init),
        grid=(batch, heads),
        in_specs=[blk(), blk(), blk(), blk(),
                  pl.BlockSpec(lam_params.shape, lambda b, h: (0, 0)),
                  pl.BlockSpec((1, LANES), lambda b, h: (0, 0))],
        out_specs=blk(),
        out_shape=jax.ShapeDtypeStruct((t, width), BF16),
        scratch_shapes=[pltpu.VMEM((n_tiles, LANES + SUM_ROWS, tq), BF16),
                        pltpu.VMEM((n_tiles, 2, LANES, tq), BF16),
                        pltpu.VMEM((n_tiles, 2, 1, tq), F32),
                        pltpu.VMEM((n_tiles, 2, LANES + SUM_ROWS, tq), F32),
                        pltpu.VMEM((2, 2, tq, tq), F32),
                        pltpu.VMEM((2, 2, tq, tq), BF16),
                        pltpu.VMEM((2, 2, 1, tq), F32)],
        compiler_params=_compiler_params(2),
        name="attn_prompt",
    )(q, k, v, g, lam_params, subln.reshape(1, LANES))


def _attn_sample_kernel(q_ref, kn_ref, vn_ref, ck_ref, cv_ref, g_ref, lam_ref, sg_ref, o_ref,
                        *, heads_per_step, lambda_init):
    lam = _lambda_value(lam_ref, lambda_init)
    sg = sg_ref[...]
    ck_all = pltpu.einshape("mud->m(ud)", ck_ref[...])
    cv_all = pltpu.einshape("mhd->m(hd)", cv_ref[...])
    for hh in range(heads_per_step):
        cs = slice(hh * LANES, (hh + 1) * LANES)
        qs = jnp.concatenate(_split_subheads(q_ref[:, cs]), axis=0).astype(BF16)
        kc = ck_all[:, cs].astype(BF16)
        vc = cv_all[:, cs].astype(BF16)
        kn = kn_ref[:, cs].astype(BF16)
        vn = vn_ref[:, cs].astype(BF16)
        sc = _dot_nt(qs, kc)
        sn = _dot_nt(qs, kn)
        m = jnp.maximum(jnp.max(sc, axis=-1, keepdims=True),
                        jnp.max(sn, axis=-1, keepdims=True))
        pc = jnp.exp2(sc - m)
        pn = jnp.exp2(sn - m)
        l = jnp.sum(pc, axis=-1, keepdims=True) + jnp.sum(pn, axis=-1, keepdims=True)
        acc = _dot(pc.astype(BF16), vc) + _dot(pn.astype(BF16), vn)
        o12 = acc * (1.0 / l)
        frames = q_ref.shape[0]
        out = _diff_finish(o12[:frames], o12[frames:], lam, sg, g_ref[:, cs].astype(F32),
                           lambda_init)
        o_ref[:, cs] = out.astype(o_ref.dtype)


def _attn_sample(q, k, v, g, cache_k, cache_v, lam_params, subln, batch, lambda_init,
                 heads_per_step):
    t, width = q.shape
    frames = t // batch
    past = cache_k.shape[0] // batch
    cols = heads_per_step * LANES
    new = lambda: pl.BlockSpec((frames, cols), lambda b, h: (b, h))
    old_k = pl.BlockSpec((past, 2 * heads_per_step, A_HEAD_DIM), lambda b, h: (b, h, 0))
    old_v = pl.BlockSpec((past, heads_per_step, 2 * A_HEAD_DIM), lambda b, h: (b, h, 0))
    return pl.pallas_call(
        functools.partial(_attn_sample_kernel, heads_per_step=heads_per_step,
                          lambda_init=lambda_init),
        grid=(batch, width // cols),
        in_specs=[new(), new(), new(), old_k, old_v, new(),
                  pl.BlockSpec(lam_params.shape, lambda b, h: (0, 0)),
                  pl.BlockSpec((1, LANES), lambda b, h: (0, 0))],
        out_specs=new(),
        out_shape=jax.ShapeDtypeStruct((t, width), BF16),
        compiler_params=_compiler_params(2),
        name="attn_sample",
    )(q, k, v, cache_k, cache_v, g, lam_params, subln.reshape(1, LANES))


def _outproj_kernel(a_ref, w_ref, x_ref, y_ref):
    y_ref[...] = x_ref[...] + _dot(a_ref[...], w_ref[...])


def _outproj_final_kernel(a_ref, w_ref, x_ref, nf_ref, y_ref):
    xo = x_ref[...] + _dot(a_ref[...], w_ref[...])
    y_ref[...] = xo * _rms_scale(xo, NORM_EPS) * nf_ref[...]


def _outproj(name, a, w_bf16, x, tm, final_norm_w=None):
    t, kdim = a.shape
    d = x.shape[1]
    in_specs = [pl.BlockSpec((tm, kdim), lambda i: (i, 0)),
                pl.BlockSpec((kdim, d), lambda i: (0, 0), pipeline_mode=pl.Buffered(1)),
                pl.BlockSpec((tm, d), lambda i: (i, 0))]
    args = [a, w_bf16, x]
    kernel_fn = _outproj_kernel
    if final_norm_w is not None:
        in_specs.append(pl.BlockSpec((1, d), lambda i: (0, 0)))
        args.append(final_norm_w.reshape(1, d))
        kernel_fn = _outproj_final_kernel
    return pl.pallas_call(
        kernel_fn,
        grid=(t // tm,),
        in_specs=in_specs,
        out_specs=pl.BlockSpec((tm, d), lambda i: (i, 0)),
        out_shape=jax.ShapeDtypeStruct((t, d), F32),
        compiler_params=_compiler_params(1),
        name=name,
    )(*args)


def _retention_kernel(*refs, rows, chunk, has_init):
    if has_init:
        lg_ref, q_ref, k_ref, v_ref, g_ref, s0_ref, o_ref, s_ref, y_ref = refs
    else:
        lg_ref, q_ref, k_ref, v_ref, g_ref, o_ref, s_ref, y_ref = refs
    n_blocks = q_ref.shape[0] // rows
    lg = lg_ref[pl.program_id(1)]
    row = lax.broadcasted_iota(jnp.int32, (rows, rows), 0)
    col = lax.broadcasted_iota(jnp.int32, (rows, rows), 1)
    dist = jnp.abs(row - col).astype(F32)
    decay = jnp.where((col // chunk) <= (row // chunk), jnp.exp(lg * dist), 0.0)
    idx = lax.broadcasted_iota(jnp.int32, (rows, 1), 0).astype(F32)
    q_decay = jnp.exp(lg * (idx + 1.0))
    k_decay = jnp.exp(lg * (rows - 1.0 - idx))
    block_decay = jnp.exp(jnp.full((1, 1), rows, F32) * lg)

    if has_init:
        s_ref[...] = s0_ref[...]
    else:
        s_ref[...] = jnp.zeros(s_ref.shape, F32)

    def mix_block(r):
        sl = slice(r * rows, (r + 1) * rows)
        q, k, v = q_ref[sl, :], k_ref[sl, :], v_ref[sl, :]
        state = s_ref[...]
        scores = _dot_nt(q, k) * decay
        y_ref[r % 2] = _dot(scores.astype(BF16), v) + q_decay * _dot(q, state.astype(BF16))
        kd = (k.astype(F32) * k_decay).astype(BF16)
        s_ref[...] = block_decay * state + _dot_tn(kd, v)

    def gate_block(r):
        sl = slice(r * rows, (r + 1) * rows)
        y = y_ref[r % 2]
        yn = y * _rms_scale(y, NORM_EPS)
        o_ref[sl, :] = (yn * _silu(g_ref[sl, :].astype(F32))).astype(o_ref.dtype)

    for r in range(n_blocks + 1):
        if r < n_blocks:
            mix_block(r)
        if r >= 1:
            gate_block(r - 1)


def _retention(name, log_gamma, q, k, v, g, batch, rows, chunk, init_state=None):
    t = q.shape[0]
    seq = t // batch
    dk = q.shape[1] // B_HEADS
    dv = v.shape[1] // B_HEADS
    in_specs = [pl.BlockSpec(memory_space=pltpu.SMEM),
                pl.BlockSpec((seq, dk), lambda b, h: (b, h)),
                pl.BlockSpec((seq, dk), lambda b, h: (b, h)),
                pl.BlockSpec((seq, dv), lambda b, h: (b, h)),
                pl.BlockSpec((seq, dv), lambda b, h: (b, h))]
    args = [log_gamma, q, k, v, g]
    state_spec = lambda: pl.BlockSpec((None, None, dk, dv), lambda b, h: (b, h, 0, 0))
    if init_state is not None:
        in_specs.append(state_spec())
        args.append(init_state)
    return pl.pallas_call(
        functools.partial(_retention_kernel, rows=rows, chunk=chunk,
                          has_init=init_state is not None),
        grid=(batch, B_HEADS),
        in_specs=in_specs,
        out_specs=[pl.BlockSpec((seq, dv), lambda b, h: (b, h)), state_spec()],
        out_shape=[jax.ShapeDtypeStruct((t, v.shape[1]), BF16),
                   jax.ShapeDtypeStruct((batch, B_HEADS, dk, dv), F32)],
        scratch_shapes=[pltpu.VMEM((2, rows, dv), F32)],
        compiler_params=_compiler_params(2),
        name=name,
    )(*args)


def _rope_tables_a(pos):
    inv = ROPE_THETA ** (-jnp.arange(0, A_ROT_DIM, 2, dtype=F32) / A_ROT_DIM)
    ang = pos[:, None] * inv[None, :]
    cos, sin = jnp.cos(ang), jnp.sin(ang)
    d = jnp.arange(LANES) % A_HEAD_DIM
    first = d < A_ROT_HALF
    second = (d >= A_ROT_HALF) & (d < A_ROT_DIM)
    idx = d % A_ROT_HALF
    cos_t = jnp.where((first | second)[None, :], cos[:, idx], 1.0)
    from_left = jnp.where(second[None, :], sin[:, idx], 0.0)
    from_right = jnp.where(first[None, :], -sin[:, idx], 0.0)
    return cos_t, from_left, from_right


def _xpos_tables_b(pos, dk):
    inv = 1.0 / (XPOS_BASE ** jnp.linspace(0.0, 1.0, dk // 2, dtype=F32))
    ang = pos[:, None] * inv[None, :]
    return jnp.cos(ang), jnp.sin(ang)


def _row_tile(t, cap):
    return cap if t % cap == 0 else t


def kernel(x_prompt, x_sample, cache_k_a, cache_v_a, state_ret, norm_a, w_in_a, lambda_q1,
           lambda_k1, lambda_q2, lambda_k2, subln_a, w_out_a, norm_b, w_in_b, w_out_b,
           norm_final):
    batch, seq, d = x_prompt.shape
    dec_batch, dec_seq, _ = x_sample.shape
    past = cache_k_a.shape[2]
    a_heads = d // (2 * A_HEAD_DIM)
    dk = d // B_HEADS
    dv = 2 * d // B_HEADS
    lambda_init = 0.8 - 0.6 * math.exp(-0.3 * 0)

    xp = x_prompt.reshape(batch * seq, d)
    xs = x_sample.reshape(dec_batch * dec_seq, d)

    w_in_a_b = w_in_a[0].astype(BF16)
    w_out_a_b = w_out_a[0].astype(BF16)
    w_in_b_b = w_in_b[0].astype(BF16)
    w_out_b_b = w_out_b[0].astype(BF16)
    lam_params = jnp.stack([lambda_q1[0], lambda_k1[0], lambda_q2[0], lambda_k2[0]])

    pos_p = jnp.arange(seq, dtype=F32)
    pos_s = jnp.tile(past + jnp.arange(dec_seq, dtype=F32), dec_batch)
    log_gamma = jnp.log1p(-jnp.exp2(-5.0 - jnp.arange(B_HEADS, dtype=F32)))

    tm_p = _row_tile(seq, ROW_TILE)
    tm_s = dec_batch * dec_seq
    tn = PROJ_COL_TILE

    a_widths = [d, d, d, d]
    a_dtypes = [BF16, BF16, F32, BF16]
    a_kernel = functools.partial(_inproj_a_kernel, q_scale=A_HEAD_DIM ** -0.5 * LOG2_E)
    k_by_head = [(1, A_HEAD_DIM)]
    qp, kp, vp, gp, khp = _inproj("inproj_a_prompt", a_kernel, xp, norm_a[0],
                                  _rope_tables_a(pos_p), w_in_a_b, a_widths, a_dtypes, tm_p, tn,
                                  by_head=k_by_head)
    qs, ks, vs, gs, khs = _inproj("inproj_a_sample", a_kernel, xs, norm_a[0],
                                  _rope_tables_a(pos_s), w_in_a_b, a_widths, a_dtypes, tm_s, tn,
                                  by_head=k_by_head)
    op = _attn_prompt(qp, kp, vp, gp, lam_params, subln_a[0], batch, seq, lambda_init,
                      tq=min(ATTN_TILE, seq))
    os_ = _attn_sample(qs, ks, vs, gs,
                       cache_k_a[0].reshape(dec_batch * past, 2 * a_heads, A_HEAD_DIM),
                       cache_v_a[0].reshape(dec_batch * past, a_heads, 2 * A_HEAD_DIM),
                       lam_params, subln_a[0], dec_batch, lambda_init, heads_per_step=SUBLANES)
    xp1 = _outproj("outproj_a_prompt", op, w_out_a_b, xp, tm_p)
    xs1 = _outproj("outproj_a_sample", os_, w_out_a_b, xs, tm_s)

    b_widths = [d, d, 2 * d, 2 * d]
    b_dtypes = [BF16, BF16, BF16, BF16]
    b_kernel = functools.partial(_inproj_b_kernel, k_scale=dk ** -0.5)
    qp2, kp2, vp2, gp2 = _inproj("inproj_b_prompt", b_kernel, xp1, norm_b[0],
                                 _xpos_tables_b(pos_p, dk), w_in_b_b, b_widths, b_dtypes, tm_p, tn)
    qs2, ks2, vs2, gs2 = _inproj("inproj_b_sample", b_kernel, xs1, norm_b[0],
                                 _xpos_tables_b(pos_s, dk), w_in_b_b, b_widths, b_dtypes, tm_s, tn)
    rp, state_p = _retention("retention_prompt", log_gamma, qp2, kp2, vp2, gp2, batch,
                             rows=min(RETENTION_ROWS, seq), chunk=CHUNK)
    rs, state_s = _retention("retention_sample", log_gamma, qs2, ks2, vs2, gs2, dec_batch,
                             rows=dec_seq, chunk=dec_seq, init_state=state_ret[0])
    yp = _outproj("outproj_b_prompt", rp, w_out_b_b, xp1, tm_p, final_norm_w=norm_final)
    ys = _outproj("outproj_b_sample", rs, w_out_b_b, xs1, tm_s, final_norm_w=norm_final)

    return (yp.reshape(batch, seq, d),
            ys.reshape(dec_batch, dec_seq, d),
            khp.reshape(1, batch, seq, 2 * a_heads, A_HEAD_DIM),
            vp.reshape(1, batch, seq, a_heads, 2 * A_HEAD_DIM),
            state_p[None],
            khs.reshape(1, dec_batch, dec_seq, 2 * a_heads, A_HEAD_DIM),
            vs.reshape(1, dec_batch, dec_seq, a_heads, 2 * A_HEAD_DIM),
            state_s[None])
```

```python
import functools
import math

import jax
import jax.numpy as jnp
from jax import lax
from jax.experimental import pallas as pl
from jax.experimental.pallas import tpu as pltpu

F32 = jnp.float32
BF16 = jnp.bfloat16

CHUNK = 64
A_HEAD_DIM = 64
A_ROT_DIM = A_HEAD_DIM // 4
A_ROT_HALF = A_ROT_DIM // 2
ROPE_THETA = 500000.0
B_HEADS = 8
XPOS_BASE = 10000.0
NORM_EPS = 1e-6
SUBLN_EPS = 1e-5

LANES = 128
SUBLANES = 8
MXU_DIM = 256
SUM_ROWS = 16
LOG2_E = math.log2(math.e)
VMEM_LIMIT_BYTES = 56 * 1024 * 1024

ROW_TILE = 512
ROW_TILE_NORMALIZED = 1024
PROJ_COL_TILE = 512
ATTN_TILE = 256
RETENTION_ROWS = 256


def _compiler_params(n_axes):
    return pltpu.CompilerParams(
        dimension_semantics=("arbitrary",) * n_axes,
        vmem_limit_bytes=VMEM_LIMIT_BYTES)


def _rms_scale(x, eps):
    return lax.rsqrt(jnp.mean(x * x, axis=-1, keepdims=True) + eps)


def _silu(g):
    h = 0.5 * g
    return h + h * jnp.tanh(h)


def _dot(a, b):
    return jnp.dot(a, b, preferred_element_type=F32)


def _dot_nt(a, b):
    return lax.dot_general(a, b, (((1,), (1,)), ((), ())), preferred_element_type=F32)


def _dot_tn(a, b):
    return lax.dot_general(a, b, (((0,), (0,)), ((), ())), preferred_element_type=F32)


def _normalize_rows(x_ref, nw_ref, xn_ref):
    @pl.when(pl.program_id(1) == 0)
    def _():
        x = x_ref[...]
        xn_ref[...] = (x * _rms_scale(x, NORM_EPS) * nw_ref[...]).astype(BF16)


def _inproj_a_kernel(x_ref, nw_ref, cos_ref, sa_ref, sb_ref, wq_ref, wk_ref, wv_ref, wg_ref,
                     q_ref, k_ref, v_ref, g_ref, kh_ref, xn_ref, *, q_scale):
    _normalize_rows(x_ref, nw_ref, xn_ref)
    xn = xn_ref[...]
    cos, sa, sb = cos_ref[...], sa_ref[...], sb_ref[...]

    def rope_store(z, out_ref, scale, heads_ref=None):
        rotated = []
        for c in range(z.shape[1] // LANES):
            zc = z[:, c * LANES:(c + 1) * LANES]
            r = (zc * cos + pltpu.roll(zc, A_ROT_HALF, 1) * sa
                 + pltpu.roll(zc, LANES - A_ROT_HALF, 1) * sb)
            if scale != 1.0:
                r = r * scale
            out_ref[:, c * LANES:(c + 1) * LANES] = r.astype(out_ref.dtype)
            rotated.append(r)
        if heads_ref is not None:
            heads_ref[...] = pltpu.einshape("m(ud)->mud", jnp.concatenate(rotated, axis=1),
                                            u=SUBLANES)

    rope_store(_dot(xn, wq_ref[...]), q_ref, q_scale)
    rope_store(_dot(xn, wk_ref[...]), k_ref, 1.0, kh_ref)
    v_ref[...] = _dot(xn, wv_ref[...])
    g_ref[...] = _dot(xn, wg_ref[...]).astype(g_ref.dtype)


def _inproj_b_kernel(xn_ref, cos_ref, sin_ref, wq_ref, wk_ref, wv_ref, wg_ref,
                     q_ref, k_ref, v_ref, g_ref, *, k_scale):
    xn = xn_ref[...]
    cos, sin = cos_ref[...], sin_ref[...]

    def rope_store(z, out_ref, scale):
        for hh in range(z.shape[1] // (2 * LANES)):
            c0 = hh * 2 * LANES
            lo = z[:, c0:c0 + LANES]
            hi = z[:, c0 + LANES:c0 + 2 * LANES]
            r_lo = lo * cos - hi * sin
            r_hi = lo * sin + hi * cos
            if scale != 1.0:
                r_lo, r_hi = r_lo * scale, r_hi * scale
            out_ref[:, c0:c0 + LANES] = r_lo.astype(out_ref.dtype)
            out_ref[:, c0 + LANES:c0 + 2 * LANES] = r_hi.astype(out_ref.dtype)

    rope_store(_dot(xn, wq_ref[...]), q_ref, 1.0)
    rope_store(_dot(xn, wk_ref[...]), k_ref, k_scale)
    v_ref[...] = _dot(xn, wv_ref[...]).astype(v_ref.dtype)
    g_ref[...] = _dot(xn, wg_ref[...]).astype(g_ref.dtype)


def _inproj(name, kernel_fn, x, norm_w, tables, w_bf16, widths, out_dtypes, tm, tn_unit,
            by_head=()):
    t, d = x.shape
    nj = min(widths) // tn_unit
    tns = [w // nj for w in widths]
    offs = [sum(widths[:s]) // tns[s] for s in range(len(widths))]
    n_pos_blocks = tables[0].shape[0] // tm

    in_specs = [pl.BlockSpec((tm, d), lambda i, j: (i, 0))]
    args = [x]
    scratch_shapes = []
    if norm_w is not None:
        in_specs.append(pl.BlockSpec((1, d), lambda i, j: (0, 0)))
        args.append(norm_w.reshape(1, d))
        scratch_shapes.append(pltpu.VMEM((tm, d), BF16))
    in_specs += [pl.BlockSpec((tm, LANES), lambda i, j: (i % n_pos_blocks, 0)) for _ in tables]
    for s in range(len(widths)):
        in_specs.append(pl.BlockSpec((d, tns[s]), functools.partial(
            lambda i, j, off: (0, off + j), off=offs[s])))
    out_specs = [pl.BlockSpec((tm, tns[s]), lambda i, j: (i, j)) for s in range(len(widths))]
    out_shape = [jax.ShapeDtypeStruct((t, widths[s]), out_dtypes[s]) for s in range(len(widths))]
    for s, head_width in by_head:
        assert tns[s] == SUBLANES * head_width
        out_specs.append(pl.BlockSpec((tm, SUBLANES, head_width), lambda i, j: (i, j, 0)))
        out_shape.append(jax.ShapeDtypeStruct((t, widths[s] // head_width, head_width), F32))
    return pl.pallas_call(
        kernel_fn,
        grid=(t // tm, nj),
        in_specs=in_specs,
        out_specs=out_specs,
        out_shape=out_shape,
        scratch_shapes=scratch_shapes,
        compiler_params=_compiler_params(2),
        name=name,
    )(*args, *tables, *([w_bf16] * len(widths)))


def _lambda_value(lam_ref, lambda_init):
    lp = lam_ref[...]
    s1 = jnp.sum(lp[0:1] * lp[1:2], axis=-1, keepdims=True)
    s2 = jnp.sum(lp[2:3] * lp[3:4], axis=-1, keepdims=True)
    return jnp.exp(s1) - jnp.exp(s2) + lambda_init


def _split_subheads(q):
    lane = lax.broadcasted_iota(jnp.int32, q.shape, 1)
    qf = q.astype(F32)
    return jnp.where(lane < A_HEAD_DIM, qf, 0.0), jnp.where(lane >= A_HEAD_DIM, qf, 0.0)


def _diff_finish(o1, o2, lam, sg, g, lambda_init):
    o = o1 - lam * o2
    on = o * _rms_scale(o, SUBLN_EPS) * sg
    on = on * (1.0 - lambda_init)
    return on * _silu(g)


def _attn_prompt_kernel(q_ref, k_ref, v_ref, g_ref, lam_ref, sg_ref, o_ref,
                        vt_ref, qt_ref, m_ref, acc_ref, s_ref, p_ref, a_ref,
                        *, tq, lambda_init):
    seq = q_ref.shape[0]
    n_tiles = seq // tq
    lam = _lambda_value(lam_ref, lambda_init)
    sg = sg_ref[...]
    ones_row = lax.broadcasted_iota(jnp.int32, (SUM_ROWS, tq), 0) == 0
    key = lax.broadcasted_iota(jnp.int32, (tq, tq), 0)
    qry = lax.broadcasted_iota(jnp.int32, (tq, tq), 1)
    diag_mask = (key // CHUNK) <= (qry // CHUNK)

    def prepare_tile(t):
        rows = slice(t * tq, (t + 1) * tq)
        vt_ref[t, 0:LANES] = v_ref[rows, :].T.astype(BF16)
        vt_ref[t, LANES:LANES + SUM_ROWS] = jnp.where(ones_row, 1.0, 0.0).astype(BF16)
        qa, qb = _split_subheads(q_ref[rows, :])
        qt_ref[t, 0] = qa.T.astype(BF16)
        qt_ref[t, 1] = qb.T.astype(BF16)
        m_ref[t] = jnp.full(m_ref.shape[1:], -jnp.inf, F32)
        acc_ref[t] = jnp.zeros(acc_ref.shape[1:], F32)

    def finish_tile(qi):
        rows = slice(qi * tq, (qi + 1) * tq)
        o1, o2 = [(acc_ref[qi, i, 0:LANES] * (1.0 / acc_ref[qi, i, LANES:LANES + 1])).T
                  for i in range(2)]
        out = _diff_finish(o1, o2, lam, sg, g_ref[rows, :].astype(F32), lambda_init)
        o_ref[rows, :] = out.astype(o_ref.dtype)

    items = [(qi, kt) for qi in range(n_tiles) for kt in range(qi + 1)]

    def score_stage(j):
        qi, kt = items[j]
        if kt == 0:
            prepare_tile(qi)
        for i in range(2):
            s_ref[j % 2, i] = _dot(k_ref[kt * tq:(kt + 1) * tq, :], qt_ref[qi, i])

    def softmax_stage(j):
        qi, kt = items[j]
        for i in range(2):
            s = s_ref[j % 2, i]
            if kt == qi:
                s = jnp.where(diag_mask, s, -jnp.inf)
            m_old = m_ref[qi, i]
            m_new = jnp.maximum(m_old, jnp.max(s, axis=0, keepdims=True))
            m_ref[qi, i] = m_new
            a_ref[j % 2, i] = jnp.exp2(m_old - m_new)
            p_ref[j % 2, i] = jnp.exp2(s - m_new).astype(BF16)

    def value_stage(j):
        qi, kt = items[j]
        for i in range(2):
            acc_ref[qi, i] = (a_ref[j % 2, i] * acc_ref[qi, i]
                              + _dot(vt_ref[kt], p_ref[j % 2, i]))
        if kt == qi:
            finish_tile(qi)

    for step in range(len(items) + 2):
        for stage, lag in ((value_stage, 2), (softmax_stage, 1), (score_stage, 0)):
            if 0 <= step - lag < len(items):
                stage(step - lag)


def _attn_prompt(q, k, v, g, lam_params, subln, batch, seq, lambda_init, tq):
    t, width = q.shape
    heads = width // LANES
    n_tiles = seq // tq
    blk = lambda: pl.BlockSpec((seq, LANES), lambda b, h: (b, h))
    return pl.pallas_call(
        functools.partial(_attn_prompt_kernel, tq=tq, lambda_init=lambda_init),
        grid=(batch, heads),
        in_specs=[blk(), blk(), blk(), blk(),
                  pl.BlockSpec(lam_params.shape, lambda b, h: (0, 0)),
                  pl.BlockSpec((1, LANES), lambda b, h: (0, 0))],
        out_specs=blk(),
        out_shape=jax.ShapeDtypeStruct((t, width), BF16),
        scratch_shapes=[pltpu.VMEM((n_tiles, LANES + SUM_ROWS, tq), BF16),
                        pltpu.VMEM((n_tiles, 2, LANES, tq), BF16),
                        pltpu.VMEM((n_tiles, 2, 1, tq), F32),
                        pltpu.VMEM((n_tiles, 2, LANES + SUM_ROWS, tq), F32),
                        pltpu.VMEM((2, 2, tq, tq), F32),
                        pltpu.VMEM((2, 2, tq, tq), BF16),
                        pltpu.VMEM((2, 2, 1, tq), F32)],
        compiler_params=_compiler_params(2),
        name="attn_prompt",
    )(q, k, v, g, lam_params, subln.reshape(1, LANES))


def _attn_sample_kernel(q_ref, kn_ref, vn_ref, ck_ref, cv_ref, g_ref, lam_ref, sg_ref, o_ref,
                        *, heads_per_step, lambda_init):
    lam = _lambda_value(lam_ref, lambda_init)
    sg = sg_ref[...]
    for hh in range(heads_per_step):
        cs = slice(hh * LANES, (hh + 1) * LANES)
        qs = jnp.concatenate(_split_subheads(q_ref[:, cs]), axis=0).astype(BF16)
        kc = ck_ref[:, cs].astype(BF16)
        vc = cv_ref[:, cs].astype(BF16)
        kn = kn_ref[:, cs].astype(BF16)
        vn = vn_ref[:, cs].astype(BF16)
        sc = _dot_nt(qs, kc)
        sn = _dot_nt(qs, kn)
        m = jnp.maximum(jnp.max(sc, axis=-1, keepdims=True),
                        jnp.max(sn, axis=-1, keepdims=True))
        pc = jnp.exp2(sc - m)
        pn = jnp.exp2(sn - m)
        l = jnp.sum(pc, axis=-1, keepdims=True) + jnp.sum(pn, axis=-1, keepdims=True)
        acc = _dot(pc.astype(BF16), vc) + _dot(pn.astype(BF16), vn)
        o12 = acc * (1.0 / l)
        frames = q_ref.shape[0]
        out = _diff_finish(o12[:frames], o12[frames:], lam, sg, g_ref[:, cs].astype(F32),
                           lambda_init)
        o_ref[:, cs] = out.astype(o_ref.dtype)


def _attn_sample(q, k, v, g, cache_k, cache_v, lam_params, subln, batch, lambda_init,
                 heads_per_step):
    t, width = q.shape
    frames = t // batch
    past = cache_k.shape[0] // batch
    cols = heads_per_step * LANES
    new = lambda: pl.BlockSpec((frames, cols), lambda b, h: (b, h))
    old = lambda: pl.BlockSpec((past, cols), lambda b, h: (b, h))
    return pl.pallas_call(
        functools.partial(_attn_sample_kernel, heads_per_step=heads_per_step,
                          lambda_init=lambda_init),
        grid=(batch, width // cols),
        in_specs=[new(), new(), new(), old(), old(), new(),
                  pl.BlockSpec(lam_params.shape, lambda b, h: (0, 0)),
                  pl.BlockSpec((1, LANES), lambda b, h: (0, 0))],
        out_specs=new(),
        out_shape=jax.ShapeDtypeStruct((t, width), BF16),
        compiler_params=_compiler_params(2),
        name="attn_sample",
    )(q, k, v, cache_k, cache_v, g, lam_params, subln.reshape(1, LANES))


def _outproj_next_kernel(a_ref, w_ref, x_ref, nw_ref, y_ref, yn_ref):
    xo = x_ref[...] + _dot(a_ref[...], w_ref[...])
    y_ref[...] = xo
    yn_ref[...] = (xo * _rms_scale(xo, NORM_EPS) * nw_ref[...]).astype(yn_ref.dtype)


def _outproj_final_kernel(a_ref, w_ref, x_ref, nf_ref, y_ref):
    xo = x_ref[...] + _dot(a_ref[...], w_ref[...])
    y_ref[...] = xo * _rms_scale(xo, NORM_EPS) * nf_ref[...]


def _outproj(name, a, w_bf16, x, tm, norm_w, final):
    t, kdim = a.shape
    d = x.shape[1]
    row_block = lambda width: pl.BlockSpec((tm, width), lambda i: (i, 0))
    out_specs = [row_block(d)]
    out_shape = [jax.ShapeDtypeStruct((t, d), F32)]
    if not final:
        out_specs.append(row_block(d))
        out_shape.append(jax.ShapeDtypeStruct((t, d), BF16))
    return pl.pallas_call(
        _outproj_final_kernel if final else _outproj_next_kernel,
        grid=(t // tm,),
        in_specs=[row_block(kdim),
                  pl.BlockSpec((kdim, d), lambda i: (0, 0), pipeline_mode=pl.Buffered(1)),
                  row_block(d),
                  pl.BlockSpec((1, d), lambda i: (0, 0))],
        out_specs=out_specs,
        out_shape=out_shape,
        compiler_params=_compiler_params(1),
        name=name,
    )(a, w_bf16, x, norm_w.reshape(1, d))


def _retention_kernel(*refs, rows, chunk, has_init):
    if has_init:
        lg_ref, q_ref, k_ref, v_ref, g_ref, s0_ref, o_ref, s_ref, y_ref = refs
    else:
        lg_ref, q_ref, k_ref, v_ref, g_ref, o_ref, s_ref, y_ref = refs
    n_blocks = q_ref.shape[0] // rows
    lg = lg_ref[pl.program_id(1)]
    row = lax.broadcasted_iota(jnp.int32, (rows, rows), 0)
    col = lax.broadcasted_iota(jnp.int32, (rows, rows), 1)
    dist = jnp.abs(row - col).astype(F32)
    decay = jnp.where((col // chunk) <= (row // chunk), jnp.exp(lg * dist), 0.0)
    idx = lax.broadcasted_iota(jnp.int32, (rows, 1), 0).astype(F32)
    q_decay = jnp.exp(lg * (idx + 1.0))
    k_decay = jnp.exp(lg * (rows - 1.0 - idx))
    block_decay = jnp.exp(jnp.full((1, 1), rows, F32) * lg)

    if has_init:
        s_ref[...] = s0_ref[...]
    else:
        s_ref[...] = jnp.zeros(s_ref.shape, F32)

    def mix_block(r):
        sl = slice(r * rows, (r + 1) * rows)
        q, k, v = q_ref[sl, :], k_ref[sl, :], v_ref[sl, :]
        state = s_ref[...]
        scores = _dot_nt(q, k) * decay
        y_ref[r % 2] = _dot(scores.astype(BF16), v) + q_decay * _dot(q, state.astype(BF16))
        kd = (k.astype(F32) * k_decay).astype(BF16)
        s_ref[...] = block_decay * state + _dot_tn(kd, v)

    def gate_block(r):
        sl = slice(r * rows, (r + 1) * rows)
        y = y_ref[r % 2]
        yn = y * _rms_scale(y, NORM_EPS)
        o_ref[sl, :] = (yn * _silu(g_ref[sl, :].astype(F32))).astype(o_ref.dtype)

    for r in range(n_blocks + 1):
        if r < n_blocks:
            mix_block(r)
        if r >= 1:
            gate_block(r - 1)


def _retention(name, log_gamma, q, k, v, g, batch, rows, chunk, init_state=None):
    t = q.shape[0]
    seq = t // batch
    dk = q.shape[1] // B_HEADS
    dv = v.shape[1] // B_HEADS
    in_specs = [pl.BlockSpec(memory_space=pltpu.SMEM),
                pl.BlockSpec((seq, dk), lambda b, h: (b, h)),
                pl.BlockSpec((seq, dk), lambda b, h: (b, h)),
                pl.BlockSpec((seq, dv), lambda b, h: (b, h)),
                pl.BlockSpec((seq, dv), lambda b, h: (b, h))]
    args = [log_gamma, q, k, v, g]
    state_spec = lambda: pl.BlockSpec((None, None, dk, dv), lambda b, h: (b, h, 0, 0))
    if init_state is not None:
        in_specs.append(state_spec())
        args.append(init_state)
    return pl.pallas_call(
        functools.partial(_retention_kernel, rows=rows, chunk=chunk,
                          has_init=init_state is not None),
        grid=(batch, B_HEADS),
        in_specs=in_specs,
        out_specs=[pl.BlockSpec((seq, dv), lambda b, h: (b, h)), state_spec()],
        out_shape=[jax.ShapeDtypeStruct((t, v.shape[1]), BF16),
                   jax.ShapeDtypeStruct((batch, B_HEADS, dk, dv), F32)],
        scratch_shapes=[pltpu.VMEM((2, rows, dv), F32)],
        compiler_params=_compiler_params(2),
        name=name,
    )(*args)


def _rope_tables_a(pos):
    inv = ROPE_THETA ** (-jnp.arange(0, A_ROT_DIM, 2, dtype=F32) / A_ROT_DIM)
    ang = pos[:, None] * inv[None, :]
    cos, sin = jnp.cos(ang), jnp.sin(ang)
    d = jnp.arange(LANES) % A_HEAD_DIM
    first = d < A_ROT_HALF
    second = (d >= A_ROT_HALF) & (d < A_ROT_DIM)
    idx = d % A_ROT_HALF
    cos_t = jnp.where((first | second)[None, :], cos[:, idx], 1.0)
    from_left = jnp.where(second[None, :], sin[:, idx], 0.0)
    from_right = jnp.where(first[None, :], -sin[:, idx], 0.0)
    return cos_t, from_left, from_right


def _xpos_tables_b(pos, dk):
    inv = 1.0 / (XPOS_BASE ** jnp.linspace(0.0, 1.0, dk // 2, dtype=F32))
    ang = pos[:, None] * inv[None, :]
    return jnp.cos(ang), jnp.sin(ang)


def _row_tile(t, cap):
    return cap if t % cap == 0 else t


def kernel(x_prompt, x_sample, cache_k_a, cache_v_a, state_ret, norm_a, w_in_a, lambda_q1,
           lambda_k1, lambda_q2, lambda_k2, subln_a, w_out_a, norm_b, w_in_b, w_out_b,
           norm_final):
    batch, seq, d = x_prompt.shape
    dec_batch, dec_seq, _ = x_sample.shape
    past = cache_k_a.shape[2]
    a_heads = d // (2 * A_HEAD_DIM)
    dk = d // B_HEADS
    dv = 2 * d // B_HEADS
    lambda_init = 0.8 - 0.6 * math.exp(-0.3 * 0)

    xp = x_prompt.reshape(batch * seq, d)
    xs = x_sample.reshape(dec_batch * dec_seq, d)

    w_in_a_b = w_in_a[0].astype(BF16)
    w_out_a_b = w_out_a[0].astype(BF16)
    w_in_b_b = w_in_b[0].astype(BF16)
    w_out_b_b = w_out_b[0].astype(BF16)
    lam_params = jnp.stack([lambda_q1[0], lambda_k1[0], lambda_q2[0], lambda_k2[0]])

    pos_p = jnp.arange(seq, dtype=F32)
    pos_s = jnp.tile(past + jnp.arange(dec_seq, dtype=F32), dec_batch)
    log_gamma = jnp.log1p(-jnp.exp2(-5.0 - jnp.arange(B_HEADS, dtype=F32)))

    tm_p = _row_tile(seq, ROW_TILE)
    tm_s = dec_batch * dec_seq
    tn = PROJ_COL_TILE

    a_widths = [d, d, d, d]
    a_dtypes = [BF16, BF16, F32, BF16]
    a_kernel = functools.partial(_inproj_a_kernel, q_scale=A_HEAD_DIM ** -0.5 * LOG2_E)
    k_by_head = [(1, A_HEAD_DIM)]
    qp, kp, vp, gp, khp = _inproj("inproj_a_prompt", a_kernel, xp, norm_a[0],
                                  _rope_tables_a(pos_p), w_in_a_b, a_widths, a_dtypes, tm_p, tn,
                                  by_head=k_by_head)
    qs, ks, vs, gs, khs = _inproj("inproj_a_sample", a_kernel, xs, norm_a[0],
                                  _rope_tables_a(pos_s), w_in_a_b, a_widths, a_dtypes, tm_s, tn,
                                  by_head=k_by_head)
    op = _attn_prompt(qp, kp, vp, gp, lam_params, subln_a[0], batch, seq, lambda_init,
                      tq=min(ATTN_TILE, seq))
    os_ = _attn_sample(qs, ks, vs, gs,
                       cache_k_a[0].reshape(dec_batch * past, d),
                       cache_v_a[0].reshape(dec_batch * past, d),
                       lam_params, subln_a[0], dec_batch, lambda_init, heads_per_step=4)
    xp1, xnp1 = _outproj("outproj_a_prompt", op, w_out_a_b, xp, tm_p, norm_b[0], final=False)
    xs1, xns1 = _outproj("outproj_a_sample", os_, w_out_a_b, xs, tm_s, norm_b[0], final=False)

    b_widths = [d, d, 2 * d, 2 * d]
    b_dtypes = [BF16, BF16, BF16, BF16]
    b_kernel = functools.partial(_inproj_b_kernel, k_scale=dk ** -0.5)
    tm_b = _row_tile(seq, ROW_TILE_NORMALIZED)
    qp2, kp2, vp2, gp2 = _inproj("inproj_b_prompt", b_kernel, xnp1, None,
                                 _xpos_tables_b(pos_p, dk), w_in_b_b, b_widths, b_dtypes, tm_b, tn)
    qs2, ks2, vs2, gs2 = _inproj("inproj_b_sample", b_kernel, xns1, None,
                                 _xpos_tables_b(pos_s, dk), w_in_b_b, b_widths, b_dtypes, tm_s, tn)
    rp, state_p = _retention("retention_prompt", log_gamma, qp2, kp2, vp2, gp2, batch,
                             rows=min(RETENTION_ROWS, seq), chunk=CHUNK)
    rs, state_s = _retention("retention_sample", log_gamma, qs2, ks2, vs2, gs2, dec_batch,
                             rows=dec_seq, chunk=dec_seq, init_state=state_ret[0])
    (yp,) = _outproj("outproj_b_prompt", rp, w_out_b_b, xp1, tm_p, norm_final, final=True)
    (ys,) = _outproj("outproj_b_sample", rs, w_out_b_b, xs1, tm_s, norm_final, final=True)

    return (yp.reshape(batch, seq, d),
            ys.reshape(dec_batch, dec_seq, d),
            khp.reshape(1, batch, seq, 2 * a_heads, A_HEAD_DIM),
            vp.reshape(1, batch, seq, a_heads, 2 * A_HEAD_DIM),
            state_p[None],
            khs.reshape(1, dec_batch, dec_seq, 2 * a_heads, A_HEAD_DIM),
            vs.reshape(1, dec_batch, dec_seq, a_heads, 2 * A_HEAD_DIM),
            state_s[None])
```

```python
import functools
import math

import jax
import jax.numpy as jnp
from jax import lax
from jax.experimental import pallas as pl
from jax.experimental.pallas import tpu as pltpu

F32 = jnp.float32
BF16 = jnp.bfloat16

CHUNK = 64
A_HEAD_DIM = 64
A_ROT_DIM = A_HEAD_DIM // 4
A_ROT_HALF = A_ROT_DIM // 2
ROPE_THETA = 500000.0
B_HEADS = 8
XPOS_BASE = 10000.0
NORM_EPS = 1e-6
SUBLN_EPS = 1e-5

LANES = 128
SUBLANES = 8
MXU_DIM = 256
SUM_ROWS = 16
LOG2_E = math.log2(math.e)
VMEM_LIMIT_BYTES = 56 * 1024 * 1024

ROW_TILE = 512
ROW_TILE_NORMALIZED = 1024
PROJ_COL_TILE = 512
ATTN_TILE = 256
RETENTION_ROWS = 256


def _compiler_params(n_axes):
    return pltpu.CompilerParams(
        dimension_semantics=("arbitrary",) * n_axes,
        vmem_limit_bytes=VMEM_LIMIT_BYTES)


def _rms_scale(x, eps):
    return lax.rsqrt(jnp.mean(x * x, axis=-1, keepdims=True) + eps)


def _silu(g):
    h = 0.5 * g
    return h + h * jnp.tanh(h)


def _dot(a, b):
    return jnp.dot(a, b, preferred_element_type=F32)


def _dot_nt(a, b):
    return lax.dot_general(a, b, (((1,), (1,)), ((), ())), preferred_element_type=F32)


def _dot_tn(a, b):
    return lax.dot_general(a, b, (((0,), (0,)), ((), ())), preferred_element_type=F32)


def _normalized_rows(x_ref, nw_ref, maybe_xn_ref):
    def normalize():
        x = x_ref[...]
        return (x * _rms_scale(x, NORM_EPS) * nw_ref[...]).astype(BF16)

    if not maybe_xn_ref:
        return normalize()
    (xn_ref,) = maybe_xn_ref

    @pl.when(pl.program_id(1) == 0)
    def _():
        xn_ref[...] = normalize()

    return xn_ref[...]


def _inproj_a_kernel(x_ref, nw_ref, cos_ref, sa_ref, sb_ref, wq_ref, wk_ref, wv_ref, wg_ref,
                     q_ref, k_ref, v_ref, g_ref, kh_ref, *maybe_xn_ref, q_scale):
    xn = _normalized_rows(x_ref, nw_ref, maybe_xn_ref)
    cos, sa, sb = cos_ref[...], sa_ref[...], sb_ref[...]

    def rope_store(z, out_ref, scale, heads_ref=None):
        rotated = []
        for c in range(z.shape[1] // LANES):
            zc = z[:, c * LANES:(c + 1) * LANES]
            r = (zc * cos + pltpu.roll(zc, A_ROT_HALF, 1) * sa
                 + pltpu.roll(zc, LANES - A_ROT_HALF, 1) * sb)
            if scale != 1.0:
                r = r * scale
            out_ref[:, c * LANES:(c + 1) * LANES] = r.astype(out_ref.dtype)
            rotated.append(r)
        if heads_ref is not None:
            heads_ref[...] = pltpu.einshape("m(ud)->mud", jnp.concatenate(rotated, axis=1),
                                            u=SUBLANES)

    rope_store(_dot(xn, wq_ref[...]), q_ref, q_scale)
    rope_store(_dot(xn, wk_ref[...]), k_ref, 1.0, kh_ref)
    v_ref[...] = _dot(xn, wv_ref[...])
    g_ref[...] = _dot(xn, wg_ref[...]).astype(g_ref.dtype)


def _inproj_b_kernel(xn_ref, cos_ref, sin_ref, wq_ref, wk_ref, wv_ref, wg_ref,
                     q_ref, k_ref, v_ref, g_ref, *, k_scale):
    xn = xn_ref[...]
    cos, sin = cos_ref[...], sin_ref[...]

    def rope_store(z, out_ref, scale):
        for hh in range(z.shape[1] // (2 * LANES)):
            c0 = hh * 2 * LANES
            lo = z[:, c0:c0 + LANES]
            hi = z[:, c0 + LANES:c0 + 2 * LANES]
            r_lo = lo * cos - hi * sin
            r_hi = lo * sin + hi * cos
            if scale != 1.0:
                r_lo, r_hi = r_lo * scale, r_hi * scale
            out_ref[:, c0:c0 + LANES] = r_lo.astype(out_ref.dtype)
            out_ref[:, c0 + LANES:c0 + 2 * LANES] = r_hi.astype(out_ref.dtype)

    rope_store(_dot(xn, wq_ref[...]), q_ref, 1.0)
    rope_store(_dot(xn, wk_ref[...]), k_ref, k_scale)
    v_ref[...] = _dot(xn, wv_ref[...]).astype(v_ref.dtype)
    g_ref[...] = _dot(xn, wg_ref[...]).astype(g_ref.dtype)


def _inproj(name, kernel_fn, x, norm_w, tables, w_bf16, widths, out_dtypes, tm, tn_unit,
            by_head=(), cols_outer=False):
    t, d = x.shape
    nj = min(widths) // tn_unit
    tns = [w // nj for w in widths]
    offs = [sum(widths[:s]) // tns[s] for s in range(len(widths))]
    n_pos_blocks = tables[0].shape[0] // tm

    def spec(block, index_map):
        if cols_outer:
            return pl.BlockSpec(block, lambda j, i: index_map(i, j))
        return pl.BlockSpec(block, index_map)

    in_specs = [spec((tm, d), lambda i, j: (i, 0))]
    args = [x]
    scratch_shapes = []
    if norm_w is not None:
        in_specs.append(spec((1, d), lambda i, j: (0, 0)))
        args.append(norm_w.reshape(1, d))
        if not cols_outer:
            scratch_shapes.append(pltpu.VMEM((tm, d), BF16))
    in_specs += [spec((tm, LANES), lambda i, j: (i % n_pos_blocks, 0)) for _ in tables]
    for s in range(len(widths)):
        in_specs.append(spec((d, tns[s]), functools.partial(
            lambda i, j, off: (0, off + j), off=offs[s])))
    out_specs = [spec((tm, tns[s]), lambda i, j: (i, j)) for s in range(len(widths))]
    out_shape = [jax.ShapeDtypeStruct((t, widths[s]), out_dtypes[s]) for s in range(len(widths))]
    for s, head_width in by_head:
        assert tns[s] == SUBLANES * head_width
        out_specs.append(spec((tm, SUBLANES, head_width), lambda i, j: (i, j, 0)))
        out_shape.append(jax.ShapeDtypeStruct((t, widths[s] // head_width, head_width), F32))
    return pl.pallas_call(
        kernel_fn,
        grid=(nj, t // tm) if cols_outer else (t // tm, nj),
        in_specs=in_specs,
        out_specs=out_specs,
        out_shape=out_shape,
        scratch_shapes=scratch_shapes,
        compiler_params=_compiler_params(2),
        name=name,
    )(*args, *tables, *([w_bf16] * len(widths)))


def _lambda_value(lam_ref, lambda_init):
    lp = lam_ref[...]
    s1 = jnp.sum(lp[0:1] * lp[1:2], axis=-1, keepdims=True)
    s2 = jnp.sum(lp[2:3] * lp[3:4], axis=-1, keepdims=True)
    return jnp.exp(s1) - jnp.exp(s2) + lambda_init


def _split_subheads(q):
    lane = lax.broadcasted_iota(jnp.int32, q.shape, 1)
    qf = q.astype(F32)
    return jnp.where(lane < A_HEAD_DIM, qf, 0.0), jnp.where(lane >= A_HEAD_DIM, qf, 0.0)


def _diff_finish(o1, o2, lam, sg, g, lambda_init):
    o = o1 - lam * o2
    on = o * _rms_scale(o, SUBLN_EPS) * sg
    on = on * (1.0 - lambda_init)
    return on * _silu(g)


def _attn_prompt_kernel(q_ref, k_ref, v_ref, g_ref, lam_ref, sg_ref, o_ref,
                        vt_ref, qt_ref, m_ref, acc_ref, s_ref, p_ref, a_ref,
                        *, tq, lambda_init):
    seq = q_ref.shape[0]
    n_tiles = seq // tq
    lam = _lambda_value(lam_ref, lambda_init)
    sg = sg_ref[...]
    ones_row = lax.broadcasted_iota(jnp.int32, (SUM_ROWS, tq), 0) == 0
    key = lax.broadcasted_iota(jnp.int32, (tq, tq), 0)
    qry = lax.broadcasted_iota(jnp.int32, (tq, tq), 1)
    diag_mask = (key // CHUNK) <= (qry // CHUNK)

    def prepare_tile(t):
        rows = slice(t * tq, (t + 1) * tq)
        vt_ref[t, 0:LANES] = v_ref[rows, :].T.astype(BF16)
        vt_ref[t, LANES:LANES + SUM_ROWS] = jnp.where(ones_row, 1.0, 0.0).astype(BF16)
        qa, qb = _split_subheads(q_ref[rows, :])
        qt_ref[t, 0] = qa.T.astype(BF16)
        qt_ref[t, 1] = qb.T.astype(BF16)
        m_ref[t] = jnp.full(m_ref.shape[1:], -jnp.inf, F32)
        acc_ref[t] = jnp.zeros(acc_ref.shape[1:], F32)

    def finish_tile(qi):
        rows = slice(qi * tq, (qi + 1) * tq)
        o1, o2 = [(acc_ref[qi, i, 0:LANES] * (1.0 / acc_ref[qi, i, LANES:LANES + 1])).T
                  for i in range(2)]
        out = _diff_finish(o1, o2, lam, sg, g_ref[rows, :].astype(F32), lambda_init)
        o_ref[rows, :] = out.astype(o_ref.dtype)

    items = [(qi, kt) for qi in range(n_tiles) for kt in range(qi + 1)]

    def score_stage(j):
        qi, kt = items[j]
        if kt == 0:
            prepare_tile(qi)
        for i in range(2):
            s_ref[j % 2, i] = _dot(k_ref[kt * tq:(kt + 1) * tq, :], qt_ref[qi, i])

    def softmax_stage(j):
        qi, kt = items[j]
        for i in range(2):
            s = s_ref[j % 2, i]
            if kt == qi:
                s = jnp.where(diag_mask, s, -jnp.inf)
            m_old = m_ref[qi, i]
            m_new = jnp.maximum(m_old, jnp.max(s, axis=0, keepdims=True))
            m_ref[qi, i] = m_new
            a_ref[j % 2, i] = jnp.exp2(m_old - m_new)
            p_ref[j % 2, i] = jnp.exp2(s - m_new).astype(BF16)

    def value_stage(j):
        qi, kt = items[j]
        for i in range(2):
            acc_ref[qi, i] = (a_ref[j % 2, i] * acc_ref[qi, i]
                              + _dot(vt_ref[kt], p_ref[j % 2, i]))
        if kt == qi:
            finish_tile(qi)

    for step in range(len(items) + 2):
        for stage, lag in ((value_stage, 2), (softmax_stage, 1), (score_stage, 0)):
            if 0 <= step - lag < len(items):
                stage(step - lag)


def _attn_prompt(q, k, v, g, lam_params, subln, batch, seq, lambda_init, tq):
    t, width = q.shape
    heads = width // LANES
    n_tiles = seq // tq
    blk = lambda: pl.BlockSpec((seq, LANES), lambda b, h: (b, h))
    return pl.pallas_call(
        functools.partial(_attn_prompt_kernel, tq=tq, lambda_init=lambda_init),
        grid=(batch, heads),
        in_specs=[blk(), blk(), blk(), blk(),
                  pl.BlockSpec(lam_params.shape, lambda b, h: (0, 0)),
                  pl.BlockSpec((1, LANES), lambda b, h: (0, 0))],
        out_specs=blk(),
        out_shape=jax.ShapeDtypeStruct((t, width), BF16),
        scratch_shapes=[pltpu.VMEM((n_tiles, LANES + SUM_ROWS, tq), BF16),
                        pltpu.VMEM((n_tiles, 2, LANES, tq), BF16),
                        pltpu.VMEM((n_tiles, 2, 1, tq), F32),
                        pltpu.VMEM((n_tiles, 2, LANES + SUM_ROWS, tq), F32),
                        pltpu.VMEM((2, 2, tq, tq), F32),
                        pltpu.VMEM((2, 2, tq, tq), BF16),
                        pltpu.VMEM((2, 2, 1, tq), F32)],
        compiler_params=_compiler_params(2),
        name="attn_prompt",
    )(q, k, v, g, lam_params, subln.reshape(1, LANES))


def _attn_sample_kernel(q_ref, kn_ref, vn_ref, ck_ref, cv_ref, g_ref, lam_ref, sg_ref, o_ref,
                        *, heads_per_step, lambda_init):
    lam = _lambda_value(lam_ref, lambda_init)
    sg = sg_ref[...]
    for hh in range(heads_per_step):
        cs = slice(hh * LANES, (hh + 1) * LANES)
        qs = jnp.concatenate(_split_subheads(q_ref[:, cs]), axis=0).astype(BF16)
        kc = ck_ref[:, cs].astype(BF16)
        vc = cv_ref[:, cs].astype(BF16)
        kn = kn_ref[:, cs].astype(BF16)
        vn = vn_ref[:, cs].astype(BF16)
        sc = _dot_nt(qs, kc)
        sn = _dot_nt(qs, kn)
        m = jnp.maximum(jnp.max(sc, axis=-1, keepdims=True),
                        jnp.max(sn, axis=-1, keepdims=True))
        pc = jnp.exp2(sc - m)
        pn = jnp.exp2(sn - m)
        l = jnp.sum(pc, axis=-1, keepdims=True) + jnp.sum(pn, axis=-1, keepdims=True)
        acc = _dot(pc.astype(BF16), vc) + _dot(pn.astype(BF16), vn)
        o12 = acc * (1.0 / l)
        frames = q_ref.shape[0]
        out = _diff_finish(o12[:frames], o12[frames:], lam, sg, g_ref[:, cs].astype(F32),
                           lambda_init)
        o_ref[:, cs] = out.astype(o_ref.dtype)


def _attn_sample(q, k, v, g, cache_k, cache_v, lam_params, subln, batch, lambda_init,
                 heads_per_step):
    t, width = q.shape
    frames = t // batch
    past = cache_k.shape[0] // batch
    cols = heads_per_step * LANES
    new = lambda: pl.BlockSpec((frames, cols), lambda b, h: (b, h))
    old = lambda: pl.BlockSpec((past, cols), lambda b, h: (b, h))
    return pl.pallas_call(
        functools.partial(_attn_sample_kernel, heads_per_step=heads_per_step,
                          lambda_init=lambda_init),
        grid=(batch, width // cols),
        in_specs=[new(), new(), new(), old(), old(), new(),
                  pl.BlockSpec(lam_params.shape, lambda b, h: (0, 0)),
                  pl.BlockSpec((1, LANES), lambda b, h: (0, 0))],
        out_specs=new(),
        out_shape=jax.ShapeDtypeStruct((t, width), BF16),
        compiler_params=_compiler_params(2),
        name="attn_sample",
    )(q, k, v, cache_k, cache_v, g, lam_params, subln.reshape(1, LANES))


def _outproj_next_kernel(a_ref, w_ref, x_ref, nw_ref, y_ref, yn_ref):
    xo = x_ref[...] + _dot(a_ref[...], w_ref[...])
    y_ref[...] = xo
    yn_ref[...] = (xo * _rms_scale(xo, NORM_EPS) * nw_ref[...]).astype(yn_ref.dtype)


def _outproj_final_kernel(a_ref, w_ref, x_ref, nf_ref, y_ref):
    xo = x_ref[...] + _dot(a_ref[...], w_ref[...])
    y_ref[...] = xo * _rms_scale(xo, NORM_EPS) * nf_ref[...]


def _outproj(name, a, w_bf16, x, tm, norm_w, final):
    t, kdim = a.shape
    d = x.shape[1]
    row_block = lambda width: pl.BlockSpec((tm, width), lambda i: (i, 0))
    out_specs = [row_block(d)]
    out_shape = [jax.ShapeDtypeStruct((t, d), F32)]
    if not final:
        out_specs.append(row_block(d))
        out_shape.append(jax.ShapeDtypeStruct((t, d), BF16))
    return pl.pallas_call(
        _outproj_final_kernel if final else _outproj_next_kernel,
        grid=(t // tm,),
        in_specs=[row_block(kdim),
                  pl.BlockSpec((kdim, d), lambda i: (0, 0), pipeline_mode=pl.Buffered(1)),
                  row_block(d),
                  pl.BlockSpec((1, d), lambda i: (0, 0))],
        out_specs=out_specs,
        out_shape=out_shape,
        compiler_params=_compiler_params(1),
        name=name,
    )(a, w_bf16, x, norm_w.reshape(1, d))


def _retention_kernel(*refs, rows, chunk, has_init):
    if has_init:
        lg_ref, q_ref, k_ref, v_ref, g_ref, s0_ref, o_ref, s_ref, y_ref = refs
    else:
        lg_ref, q_ref, k_ref, v_ref, g_ref, o_ref, s_ref, y_ref = refs
    n_blocks = q_ref.shape[0] // rows
    lg = lg_ref[pl.program_id(1)]
    row = lax.broadcasted_iota(jnp.int32, (rows, rows), 0)
    col = lax.broadcasted_iota(jnp.int32, (rows, rows), 1)
    dist = jnp.abs(row - col).astype(F32)
    decay = jnp.where((col // chunk) <= (row // chunk), jnp.exp(lg * dist), 0.0)
    idx = lax.broadcasted_iota(jnp.int32, (rows, 1), 0).astype(F32)
    q_decay = jnp.exp(lg * (idx + 1.0))
    k_decay = jnp.exp(lg * (rows - 1.0 - idx))
    block_decay = jnp.exp(jnp.full((1, 1), rows, F32) * lg)

    if has_init:
        s_ref[...] = s0_ref[...]
    else:
        s_ref[...] = jnp.zeros(s_ref.shape, F32)

    def mix_block(r):
        sl = slice(r * rows, (r + 1) * rows)
        q, k, v = q_ref[sl, :], k_ref[sl, :], v_ref[sl, :]
        state = s_ref[...]
        scores = _dot_nt(q, k) * decay
        y_ref[r % 2] = _dot(scores.astype(BF16), v) + q_decay * _dot(q, state.astype(BF16))
        kd = (k.astype(F32) * k_decay).astype(BF16)
        s_ref[...] = block_decay * state + _dot_tn(kd, v)

    def gate_block(r):
        sl = slice(r * rows, (r + 1) * rows)
        y = y_ref[r % 2]
        yn = y * _rms_scale(y, NORM_EPS)
        o_ref[sl, :] = (yn * _silu(g_ref[sl, :].astype(F32))).astype(o_ref.dtype)

    for r in range(n_blocks + 1):
        if r < n_blocks:
            mix_block(r)
        if r >= 1:
            gate_block(r - 1)


def _retention(name, log_gamma, q, k, v, g, batch, rows, chunk, init_state=None):
    t = q.shape[0]
    seq = t // batch
    dk = q.shape[1] // B_HEADS
    dv = v.shape[1] // B_HEADS
    in_specs = [pl.BlockSpec(memory_space=pltpu.SMEM),
                pl.BlockSpec((seq, dk), lambda b, h: (b, h)),
                pl.BlockSpec((seq, dk), lambda b, h: (b, h)),
                pl.BlockSpec((seq, dv), lambda b, h: (b, h)),
                pl.BlockSpec((seq, dv), lambda b, h: (b, h))]
    args = [log_gamma, q, k, v, g]
    state_spec = lambda: pl.BlockSpec((None, None, dk, dv), lambda b, h: (b, h, 0, 0))
    if init_state is not None:
        in_specs.append(state_spec())
        args.append(init_state)
    return pl.pallas_call(
        functools.partial(_retention_kernel, rows=rows, chunk=chunk,
                          has_init=init_state is not None),
        grid=(batch, B_HEADS),
        in_specs=in_specs,
        out_specs=[pl.BlockSpec((seq, dv), lambda b, h: (b, h)), state_spec()],
        out_shape=[jax.ShapeDtypeStruct((t, v.shape[1]), BF16),
                   jax.ShapeDtypeStruct((batch, B_HEADS, dk, dv), F32)],
        scratch_shapes=[pltpu.VMEM((2, rows, dv), F32)],
        compiler_params=_compiler_params(2),
        name=name,
    )(*args)


def _rope_tables_a(pos):
    inv = ROPE_THETA ** (-jnp.arange(0, A_ROT_DIM, 2, dtype=F32) / A_ROT_DIM)
    ang = pos[:, None] * inv[None, :]
    cos, sin = jnp.cos(ang), jnp.sin(ang)
    d = jnp.arange(LANES) % A_HEAD_DIM
    first = d < A_ROT_HALF
    second = (d >= A_ROT_HALF) & (d < A_ROT_DIM)
    idx = d % A_ROT_HALF
    cos_t = jnp.where((first | second)[None, :], cos[:, idx], 1.0)
    from_left = jnp.where(second[None, :], sin[:, idx], 0.0)
    from_right = jnp.where(first[None, :], -sin[:, idx], 0.0)
    return cos_t, from_left, from_right


def _xpos_tables_b(pos, dk):
    inv = 1.0 / (XPOS_BASE ** jnp.linspace(0.0, 1.0, dk // 2, dtype=F32))
    ang = pos[:, None] * inv[None, :]
    return jnp.cos(ang), jnp.sin(ang)


def _row_tile(t, cap):
    return cap if t % cap == 0 else t


def kernel(x_prompt, x_sample, cache_k_a, cache_v_a, state_ret, norm_a, w_in_a, lambda_q1,
           lambda_k1, lambda_q2, lambda_k2, subln_a, w_out_a, norm_b, w_in_b, w_out_b,
           norm_final):
    batch, seq, d = x_prompt.shape
    dec_batch, dec_seq, _ = x_sample.shape
    past = cache_k_a.shape[2]
    a_heads = d // (2 * A_HEAD_DIM)
    dk = d // B_HEADS
    dv = 2 * d // B_HEADS
    lambda_init = 0.8 - 0.6 * math.exp(-0.3 * 0)

    xp = x_prompt.reshape(batch * seq, d)
    xs = x_sample.reshape(dec_batch * dec_seq, d)

    w_in_a_b = w_in_a[0].astype(BF16)
    w_out_a_b = w_out_a[0].astype(BF16)
    w_in_b_b = w_in_b[0].astype(BF16)
    w_out_b_b = w_out_b[0].astype(BF16)
    lam_params = jnp.stack([lambda_q1[0], lambda_k1[0], lambda_q2[0], lambda_k2[0]])

    pos_p = jnp.arange(seq, dtype=F32)
    pos_s = jnp.tile(past + jnp.arange(dec_seq, dtype=F32), dec_batch)
    log_gamma = jnp.log1p(-jnp.exp2(-5.0 - jnp.arange(B_HEADS, dtype=F32)))

    tm_p = _row_tile(seq, ROW_TILE)
    tm_b = _row_tile(seq, ROW_TILE_NORMALIZED)
    tm_s = dec_batch * dec_seq
    tn = PROJ_COL_TILE

    a_widths = [d, d, d, d]
    a_dtypes = [BF16, BF16, F32, BF16]
    a_kernel = functools.partial(_inproj_a_kernel, q_scale=A_HEAD_DIM ** -0.5 * LOG2_E)
    k_by_head = [(1, A_HEAD_DIM)]
    qp, kp, vp, gp, khp = _inproj("inproj_a_prompt", a_kernel, xp, norm_a[0],
                                  _rope_tables_a(pos_p), w_in_a_b, a_widths, a_dtypes, tm_p, tn,
                                  by_head=k_by_head, cols_outer=True)
    qs, ks, vs, gs, khs = _inproj("inproj_a_sample", a_kernel, xs, norm_a[0],
                                  _rope_tables_a(pos_s), w_in_a_b, a_widths, a_dtypes, tm_s, tn,
                                  by_head=k_by_head)
    op = _attn_prompt(qp, kp, vp, gp, lam_params, subln_a[0], batch, seq, lambda_init,
                      tq=min(ATTN_TILE, seq))
    os_ = _attn_sample(qs, ks, vs, gs,
                       cache_k_a[0].astype(BF16).reshape(dec_batch * past, d),
                       cache_v_a[0].astype(BF16).reshape(dec_batch * past, d),
                       lam_params, subln_a[0], dec_batch, lambda_init, heads_per_step=4)
    xp1, xnp1 = _outproj("outproj_a_prompt", op, w_out_a_b, xp, tm_p, norm_b[0], final=False)
    xs1, xns1 = _outproj("outproj_a_sample", os_, w_out_a_b, xs, tm_s, norm_b[0], final=False)

    b_widths = [d, d, 2 * d, 2 * d]
    b_dtypes = [BF16, BF16, BF16, BF16]
    b_kernel = functools.partial(_inproj_b_kernel, k_scale=dk ** -0.5)
    qp2, kp2, vp2, gp2 = _inproj("inproj_b_prompt", b_kernel, xnp1, None,
                                 _xpos_tables_b(pos_p, dk), w_in_b_b, b_widths, b_dtypes, tm_b, tn)
    qs2, ks2, vs2, gs2 = _inproj("inproj_b_sample", b_kernel, xns1, None,
                                 _xpos_tables_b(pos_s, dk), w_in_b_b, b_widths, b_dtypes, tm_s, tn)
    rp, state_p = _retention("retention_prompt", log_gamma, qp2, kp2, vp2, gp2, batch,
                             rows=min(RETENTION_ROWS, seq), chunk=CHUNK)
    rs, state_s = _retention("retention_sample", log_gamma, qs2, ks2, vs2, gs2, dec_batch,
                             rows=dec_seq, chunk=dec_seq, init_state=state_ret[0])
    (yp,) = _outproj("outproj_b_prompt", rp, w_out_b_b, xp1, tm_p, norm_final, final=True)
    (ys,) = _outproj("outproj_b_sample", rs, w_out_b_b, xs1, tm_s, norm_final, final=True)

    return (yp.reshape(batch, seq, d),
            ys.reshape(dec_batch, dec_seq, d),
            khp.reshape(1, batch, seq, 2 * a_heads, A_HEAD_DIM),
            vp.reshape(1, batch, seq, a_heads, 2 * A_HEAD_DIM),
            state_p[None],
            khs.reshape(1, dec_batch, dec_seq, 2 * a_heads, A_HEAD_DIM),
            vs.reshape(1, dec_batch, dec_seq, a_heads, 2 * A_HEAD_DIM),
            state_s[None])
```

```python
import functools
import math

import jax
import jax.numpy as jnp
from jax import lax
from jax.experimental import pallas as pl
from jax.experimental.pallas import tpu as pltpu

F32 = jnp.float32
BF16 = jnp.bfloat16

CHUNK = 64
A_HEAD_DIM = 64
A_ROT_DIM = A_HEAD_DIM // 4
A_ROT_HALF = A_ROT_DIM // 2
ROPE_THETA = 500000.0
B_HEADS = 8
XPOS_BASE = 10000.0
NORM_EPS = 1e-6
SUBLN_EPS = 1e-5

LANES = 128
SUBLANES = 8
MXU_DIM = 256
SUM_ROWS = 16
LOG2_E = math.log2(math.e)
VMEM_LIMIT_BYTES = 56 * 1024 * 1024

ROW_TILE = 512
ROW_TILE_NORMALIZED = 1024
PROJ_COL_TILE = 512
ATTN_TILE = 256
RETENTION_ROWS = 256


def _compiler_params(n_axes):
    return pltpu.CompilerParams(
        dimension_semantics=("arbitrary",) * n_axes,
        vmem_limit_bytes=VMEM_LIMIT_BYTES)


def _rms_scale(x, eps):
    return lax.rsqrt(jnp.mean(x * x, axis=-1, keepdims=True) + eps)


def _silu(g):
    h = 0.5 * g
    return h + h * jnp.tanh(h)


def _dot(a, b):
    return jnp.dot(a, b, preferred_element_type=F32)


def _dot_nt(a, b):
    return lax.dot_general(a, b, (((1,), (1,)), ((), ())), preferred_element_type=F32)


def _dot_tn(a, b):
    return lax.dot_general(a, b, (((0,), (0,)), ((), ())), preferred_element_type=F32)


def _normalized_rows(x_ref, nw_ref, maybe_xn_ref):
    def normalize():
        x = x_ref[...]
        return (x * _rms_scale(x, NORM_EPS) * nw_ref[...]).astype(BF16)

    if not maybe_xn_ref:
        return normalize()
    (xn_ref,) = maybe_xn_ref

    @pl.when(pl.program_id(1) == 0)
    def _():
        xn_ref[...] = normalize()

    return xn_ref[...]


def _inproj_a_kernel(x_ref, nw_ref, cos_ref, sa_ref, sb_ref, wq_ref, wk_ref, wv_ref, wg_ref,
                     q_ref, k_ref, v_ref, g_ref, kh_ref, *maybe_xn_ref, q_scale):
    xn = _normalized_rows(x_ref, nw_ref, maybe_xn_ref)
    cos, sa, sb = cos_ref[...], sa_ref[...], sb_ref[...]

    def rope_store(z, out_ref, scale, heads_ref=None):
        rotated = []
        for c in range(z.shape[1] // LANES):
            zc = z[:, c * LANES:(c + 1) * LANES]
            r = (zc * cos + pltpu.roll(zc, A_ROT_HALF, 1) * sa
                 + pltpu.roll(zc, LANES - A_ROT_HALF, 1) * sb)
            if scale != 1.0:
                r = r * scale
            out_ref[:, c * LANES:(c + 1) * LANES] = r.astype(out_ref.dtype)
            rotated.append(r)
        if heads_ref is not None:
            heads_ref[...] = pltpu.einshape("m(ud)->mud", jnp.concatenate(rotated, axis=1),
                                            u=SUBLANES)

    rope_store(_dot(xn, wq_ref[...]), q_ref, q_scale)
    rope_store(_dot(xn, wk_ref[...]), k_ref, 1.0, kh_ref)
    v_ref[...] = _dot(xn, wv_ref[...])
    g_ref[...] = _dot(xn, wg_ref[...]).astype(g_ref.dtype)


def _inproj_b_kernel(xn_ref, cos_ref, sin_ref, wq_ref, wk_ref, wv_ref, wg_ref,
                     q_ref, k_ref, v_ref, g_ref, *, k_scale):
    xn = xn_ref[...]
    cos, sin = cos_ref[...], sin_ref[...]

    def rope_store(z, out_ref, scale):
        for hh in range(z.shape[1] // (2 * LANES)):
            c0 = hh * 2 * LANES
            lo = z[:, c0:c0 + LANES]
            hi = z[:, c0 + LANES:c0 + 2 * LANES]
            r_lo = lo * cos - hi * sin
            r_hi = lo * sin + hi * cos
            if scale != 1.0:
                r_lo, r_hi = r_lo * scale, r_hi * scale
            out_ref[:, c0:c0 + LANES] = r_lo.astype(out_ref.dtype)
            out_ref[:, c0 + LANES:c0 + 2 * LANES] = r_hi.astype(out_ref.dtype)

    rope_store(_dot(xn, wq_ref[...]), q_ref, 1.0)
    rope_store(_dot(xn, wk_ref[...]), k_ref, k_scale)
    v_ref[...] = _dot(xn, wv_ref[...]).astype(v_ref.dtype)
    g_ref[...] = _dot(xn, wg_ref[...]).astype(g_ref.dtype)


def _inproj(name, kernel_fn, x, norm_w, tables, w_bf16, widths, out_dtypes, tm, tn_unit,
            by_head=(), cols_outer=False):
    t, d = x.shape
    nj = min(widths) // tn_unit
    tns = [w // nj for w in widths]
    offs = [sum(widths[:s]) // tns[s] for s in range(len(widths))]
    n_pos_blocks = tables[0].shape[0] // tm

    def spec(block, index_map):
        if cols_outer:
            return pl.BlockSpec(block, lambda j, i: index_map(i, j))
        return pl.BlockSpec(block, index_map)

    in_specs = [spec((tm, d), lambda i, j: (i, 0))]
    args = [x]
    scratch_shapes = []
    if norm_w is not None:
        in_specs.append(spec((1, d), lambda i, j: (0, 0)))
        args.append(norm_w.reshape(1, d))
        if not cols_outer:
            scratch_shapes.append(pltpu.VMEM((tm, d), BF16))
    in_specs += [spec((tm, LANES), lambda i, j: (i % n_pos_blocks, 0)) for _ in tables]
    for s in range(len(widths)):
        in_specs.append(spec((d, tns[s]), functools.partial(
            lambda i, j, off: (0, off + j), off=offs[s])))
    out_specs = [spec((tm, tns[s]), lambda i, j: (i, j)) for s in range(len(widths))]
    out_shape = [jax.ShapeDtypeStruct((t, widths[s]), out_dtypes[s]) for s in range(len(widths))]
    for s, head_width in by_head:
        assert tns[s] == SUBLANES * head_width
        out_specs.append(spec((tm, SUBLANES, head_width), lambda i, j: (i, j, 0)))
        out_shape.append(jax.ShapeDtypeStruct((t, widths[s] // head_width, head_width), F32))
    return pl.pallas_call(
        kernel_fn,
        grid=(nj, t // tm) if cols_outer else (t // tm, nj),
        in_specs=in_specs,
        out_specs=out_specs,
        out_shape=out_shape,
        scratch_shapes=scratch_shapes,
        compiler_params=_compiler_params(2),
        name=name,
    )(*args, *tables, *([w_bf16] * len(widths)))


def _lambda_value(lam_ref, lambda_init):
    lp = lam_ref[...]
    s1 = jnp.sum(lp[0:1] * lp[1:2], axis=-1, keepdims=True)
    s2 = jnp.sum(lp[2:3] * lp[3:4], axis=-1, keepdims=True)
    return jnp.exp(s1) - jnp.exp(s2) + lambda_init


def _split_subheads(q):
    lane = lax.broadcasted_iota(jnp.int32, q.shape, 1)
    qf = q.astype(F32)
    return jnp.where(lane < A_HEAD_DIM, qf, 0.0), jnp.where(lane >= A_HEAD_DIM, qf, 0.0)


def _diff_finish(o1, o2, lam, sg, g, lambda_init):
    o = o1 - lam * o2
    on = o * _rms_scale(o, SUBLN_EPS) * sg
    on = on * (1.0 - lambda_init)
    return on * _silu(g)


def _attn_prompt_kernel(q_ref, k_ref, v_ref, g_ref, lam_ref, sg_ref, o_ref,
                        vt_ref, qt_ref, m_ref, acc_ref, s_ref, p_ref, a_ref,
                        *, tq, lambda_init):
    seq = q_ref.shape[0]
    n_tiles = seq // tq
    lam = _lambda_value(lam_ref, lambda_init)
    sg = sg_ref[...]
    ones_row = lax.broadcasted_iota(jnp.int32, (SUM_ROWS, tq), 0) == 0
    key = lax.broadcasted_iota(jnp.int32, (tq, tq), 0)
    qry = lax.broadcasted_iota(jnp.int32, (tq, tq), 1)
    diag_mask = (key // CHUNK) <= (qry // CHUNK)

    def prepare_tile(t):
        rows = slice(t * tq, (t + 1) * tq)
        vt_ref[t, 0:LANES] = v_ref[rows, :].T.astype(BF16)
        vt_ref[t, LANES:LANES + SUM_ROWS] = jnp.where(ones_row, 1.0, 0.0).astype(BF16)
        qa, qb = _split_subheads(q_ref[rows, :])
        qt_ref[t, 0] = qa.T.astype(BF16)
        qt_ref[t, 1] = qb.T.astype(BF16)
        m_ref[t] = jnp.full(m_ref.shape[1:], -jnp.inf, F32)
        acc_ref[t] = jnp.zeros(acc_ref.shape[1:], F32)

    def finish_tile(qi):
        rows = slice(qi * tq, (qi + 1) * tq)
        o1, o2 = [(acc_ref[qi, i, 0:LANES] * (1.0 / acc_ref[qi, i, LANES:LANES + 1])).T
                  for i in range(2)]
        out = _diff_finish(o1, o2, lam, sg, g_ref[rows, :].astype(F32), lambda_init)
        o_ref[rows, :] = out.astype(o_ref.dtype)

    items = [(qi, kt) for qi in range(n_tiles) for kt in range(qi + 1)]

    def score_stage(j):
        qi, kt = items[j]
        if kt == 0:
            prepare_tile(qi)
        for i in range(2):
            s_ref[j % 2, i] = _dot(k_ref[kt * tq:(kt + 1) * tq, :], qt_ref[qi, i])

    def softmax_stage(j):
        qi, kt = items[j]
        for i in range(2):
            s = s_ref[j % 2, i]
            if kt == qi:
                s = jnp.where(diag_mask, s, -jnp.inf)
            m_old = m_ref[qi, i]
            m_new = jnp.maximum(m_old, jnp.max(s, axis=0, keepdims=True))
            m_ref[qi, i] = m_new
            a_ref[j % 2, i] = jnp.exp2(m_old - m_new)
            p_ref[j % 2, i] = jnp.exp2(s - m_new).astype(BF16)

    def value_stage(j):
        qi, kt = items[j]
        for i in range(2):
            acc_ref[qi, i] = (a_ref[j % 2, i] * acc_ref[qi, i]
                              + _dot(vt_ref[kt], p_ref[j % 2, i]))
        if kt == qi:
            finish_tile(qi)

    for step in range(len(items) + 2):
        for stage, lag in ((value_stage, 2), (softmax_stage, 1), (score_stage, 0)):
            if 0 <= step - lag < len(items):
                stage(step - lag)


def _attn_prompt(q, k, v, g, lam_params, subln, batch, seq, lambda_init, tq):
    t, width = q.shape
    heads = width // LANES
    n_tiles = seq // tq
    blk = lambda: pl.BlockSpec((seq, LANES), lambda b, h: (b, h))
    return pl.pallas_call(
        functools.partial(_attn_prompt_kernel, tq=tq, lambda_init=lambda_init),
        grid=(batch, heads),
        in_specs=[blk(), blk(), blk(), blk(),
                  pl.BlockSpec(lam_params.shape, lambda b, h: (0, 0)),
                  pl.BlockSpec((1, LANES), lambda b, h: (0, 0))],
        out_specs=blk(),
        out_shape=jax.ShapeDtypeStruct((t, width), BF16),
        scratch_shapes=[pltpu.VMEM((n_tiles, LANES + SUM_ROWS, tq), BF16),
                        pltpu.VMEM((n_tiles, 2, LANES, tq), BF16),
                        pltpu.VMEM((n_tiles, 2, 1, tq), F32),
                        pltpu.VMEM((n_tiles, 2, LANES + SUM_ROWS, tq), F32),
                        pltpu.VMEM((2, 2, tq, tq), F32),
                        pltpu.VMEM((2, 2, tq, tq), BF16),
                        pltpu.VMEM((2, 2, 1, tq), F32)],
        compiler_params=_compiler_params(2),
        name="attn_prompt",
    )(q, k, v, g, lam_params, subln.reshape(1, LANES))


def _attn_sample_kernel(q_ref, kn_ref, vn_ref, ck_ref, cv_ref, g_ref, lam_ref, sg_ref, o_ref,
                        *, heads_per_step, lambda_init):
    lam = _lambda_value(lam_ref, lambda_init)
    sg = sg_ref[...]
    for hh in range(heads_per_step):
        cs = slice(hh * LANES, (hh + 1) * LANES)
        qs = jnp.concatenate(_split_subheads(q_ref[:, cs]), axis=0).astype(BF16)
        kc = ck_ref[:, cs].astype(BF16)
        vc = cv_ref[:, cs].astype(BF16)
        kn = kn_ref[:, cs].astype(BF16)
        vn = vn_ref[:, cs].astype(BF16)
        sc = _dot_nt(qs, kc)
        sn = _dot_nt(qs, kn)
        m = jnp.maximum(jnp.max(sc, axis=-1, keepdims=True),
                        jnp.max(sn, axis=-1, keepdims=True))
        pc = jnp.exp2(sc - m)
        pn = jnp.exp2(sn - m)
        l = jnp.sum(pc, axis=-1, keepdims=True) + jnp.sum(pn, axis=-1, keepdims=True)
        acc = _dot(pc.astype(BF16), vc) + _dot(pn.astype(BF16), vn)
        o12 = acc * (1.0 / l)
        frames = q_ref.shape[0]
        out = _diff_finish(o12[:frames], o12[frames:], lam, sg, g_ref[:, cs].astype(F32),
                           lambda_init)
        o_ref[:, cs] = out.astype(o_ref.dtype)


def _attn_sample(q, k, v, g, cache_k, cache_v, lam_params, subln, batch, lambda_init,
                 heads_per_step):
    t, width = q.shape
    frames = t // batch
    past = cache_k.shape[0] // batch
    cols = heads_per_step * LANES
    new = lambda: pl.BlockSpec((frames, cols), lambda b, h: (b, h))
    old = lambda: pl.BlockSpec((past, cols), lambda b, h: (b, h))
    return pl.pallas_call(
        functools.partial(_attn_sample_kernel, heads_per_step=heads_per_step,
                          lambda_init=lambda_init),
        grid=(batch, width // cols),
        in_specs=[new(), new(), new(), old(), old(), new(),
                  pl.BlockSpec(lam_params.shape, lambda b, h: (0, 0)),
                  pl.BlockSpec((1, LANES), lambda b, h: (0, 0))],
        out_specs=new(),
        out_shape=jax.ShapeDtypeStruct((t, width), BF16),
        compiler_params=_compiler_params(2),
        name="attn_sample",
    )(q, k, v, cache_k, cache_v, g, lam_params, subln.reshape(1, LANES))


def _outproj_next_kernel(a_ref, w_ref, x_ref, nw_ref, y_ref, yn_ref):
    xo = x_ref[...] + _dot(a_ref[...], w_ref[...])
    y_ref[...] = xo
    yn_ref[...] = (xo * _rms_scale(xo, NORM_EPS) * nw_ref[...]).astype(yn_ref.dtype)


def _outproj_final_kernel(a_ref, w_ref, x_ref, nf_ref, y_ref):
    xo = x_ref[...] + _dot(a_ref[...], w_ref[...])
    y_ref[...] = xo * _rms_scale(xo, NORM_EPS) * nf_ref[...]


def _outproj(name, a, w_bf16, x, tm, norm_w, final):
    t, kdim = a.shape
    d = x.shape[1]
    row_block = lambda width: pl.BlockSpec((tm, width), lambda i: (i, 0))
    out_specs = [row_block(d)]
    out_shape = [jax.ShapeDtypeStruct((t, d), F32)]
    if not final:
        out_specs.append(row_block(d))
        out_shape.append(jax.ShapeDtypeStruct((t, d), BF16))
    return pl.pallas_call(
        _outproj_final_kernel if final else _outproj_next_kernel,
        grid=(t // tm,),
        in_specs=[row_block(kdim),
                  pl.BlockSpec((kdim, d), lambda i: (0, 0), pipeline_mode=pl.Buffered(1)),
                  row_block(d),
                  pl.BlockSpec((1, d), lambda i: (0, 0))],
        out_specs=out_specs,
        out_shape=out_shape,
        compiler_params=_compiler_params(1),
        name=name,
    )(a, w_bf16, x, norm_w.reshape(1, d))


def _retention_kernel(*refs, rows, chunk, has_init):
    if has_init:
        lg_ref, q_ref, k_ref, v_ref, g_ref, s0_ref, o_ref, s_ref, y_ref = refs
    else:
        lg_ref, q_ref, k_ref, v_ref, g_ref, o_ref, s_ref, y_ref = refs
    n_blocks = q_ref.shape[0] // rows
    lg = lg_ref[pl.program_id(1)]
    row = lax.broadcasted_iota(jnp.int32, (rows, rows), 0)
    col = lax.broadcasted_iota(jnp.int32, (rows, rows), 1)
    dist = jnp.abs(row - col).astype(F32)
    decay = jnp.where((col // chunk) <= (row // chunk), jnp.exp(lg * dist), 0.0)
    idx = lax.broadcasted_iota(jnp.int32, (rows, 1), 0).astype(F32)
    q_decay = jnp.exp(lg * (idx + 1.0))
    k_decay = jnp.exp(lg * (rows - 1.0 - idx))
    block_decay = jnp.exp(jnp.full((1, 1), rows, F32) * lg)

    if has_init:
        s_ref[...] = s0_ref[...]
    else:
        s_ref[...] = jnp.zeros(s_ref.shape, F32)

    def mix_block(r):
        sl = slice(r * rows, (r + 1) * rows)
        q, k, v = q_ref[sl, :], k_ref[sl, :], v_ref[sl, :]
        state = s_ref[...]
        scores = _dot_nt(q, k) * decay
        y_ref[r % 2] = _dot(scores.astype(BF16), v) + q_decay * _dot(q, state.astype(BF16))
        kd = (k.astype(F32) * k_decay).astype(BF16)
        s_ref[...] = block_decay * state + _dot_tn(kd, v)

    def gate_block(r):
        sl = slice(r * rows, (r + 1) * rows)
        y = y_ref[r % 2]
        yn = y * _rms_scale(y, NORM_EPS)
        o_ref[sl, :] = (yn * _silu(g_ref[sl, :].astype(F32))).astype(o_ref.dtype)

    for r in range(n_blocks + 1):
        if r < n_blocks:
            mix_block(r)
        if r >= 1:
            gate_block(r - 1)


def _retention(name, log_gamma, q, k, v, g, batch, rows, chunk, init_state=None):
    t = q.shape[0]
    seq = t // batch
    dk = q.shape[1] // B_HEADS
    dv = v.shape[1] // B_HEADS
    in_specs = [pl.BlockSpec(memory_space=pltpu.SMEM),
                pl.BlockSpec((seq, dk), lambda b, h: (b, h)),
                pl.BlockSpec((seq, dk), lambda b, h: (b, h)),
                pl.BlockSpec((seq, dv), lambda b, h: (b, h)),
                pl.BlockSpec((seq, dv), lambda b, h: (b, h))]
    args = [log_gamma, q, k, v, g]
    state_spec = lambda: pl.BlockSpec((None, None, dk, dv), lambda b, h: (b, h, 0, 0))
    if init_state is not None:
        in_specs.append(state_spec())
        args.append(init_state)
    return pl.pallas_call(
        functools.partial(_retention_kernel, rows=rows, chunk=chunk,
                          has_init=init_state is not None),
        grid=(batch, B_HEADS),
        in_specs=in_specs,
        out_specs=[pl.BlockSpec((seq, dv), lambda b, h: (b, h)), state_spec()],
        out_shape=[jax.ShapeDtypeStruct((t, v.shape[1]), BF16),
                   jax.ShapeDtypeStruct((batch, B_HEADS, dk, dv), F32)],
        scratch_shapes=[pltpu.VMEM((2, rows, dv), F32)],
        compiler_params=_compiler_params(2),
        name=name,
    )(*args)


def _rope_tables_a(pos):
    inv = ROPE_THETA ** (-jnp.arange(0, A_ROT_DIM, 2, dtype=F32) / A_ROT_DIM)
    ang = pos[:, None] * inv[None, :]
    cos, sin = jnp.cos(ang), jnp.sin(ang)
    d = jnp.arange(LANES) % A_HEAD_DIM
    first = d < A_ROT_HALF
    second = (d >= A_ROT_HALF) & (d < A_ROT_DIM)
    idx = d % A_ROT_HALF
    cos_t = jnp.where((first | second)[None, :], cos[:, idx], 1.0)
    from_left = jnp.where(second[None, :], sin[:, idx], 0.0)
    from_right = jnp.where(first[None, :], -sin[:, idx], 0.0)
    return cos_t, from_left, from_right


def _xpos_tables_b(pos, dk):
    inv = 1.0 / (XPOS_BASE ** jnp.linspace(0.0, 1.0, dk // 2, dtype=F32))
    ang = pos[:, None] * inv[None, :]
    return jnp.cos(ang), jnp.sin(ang)


def _row_tile(t, cap):
    return cap if t % cap == 0 else t


def kernel(x_prompt, x_sample, cache_k_a, cache_v_a, state_ret, norm_a, w_in_a, lambda_q1,
           lambda_k1, lambda_q2, lambda_k2, subln_a, w_out_a, norm_b, w_in_b, w_out_b,
           norm_final):
    batch, seq, d = x_prompt.shape
    dec_batch, dec_seq, _ = x_sample.shape
    past = cache_k_a.shape[2]
    a_heads = d // (2 * A_HEAD_DIM)
    dk = d // B_HEADS
    dv = 2 * d // B_HEADS
    lambda_init = 0.8 - 0.6 * math.exp(-0.3 * 0)

    xp = x_prompt.reshape(batch * seq, d)
    xs = x_sample.reshape(dec_batch * dec_seq, d)

    w_in_a_b = w_in_a[0].astype(BF16)
    w_out_a_b = w_out_a[0].astype(BF16)
    w_in_b_b = w_in_b[0].astype(BF16)
    w_out_b_b = w_out_b[0].astype(BF16)
    lam_params = jnp.stack([lambda_q1[0], lambda_k1[0], lambda_q2[0], lambda_k2[0]])

    pos_p = jnp.arange(seq, dtype=F32)
    pos_s = jnp.tile(past + jnp.arange(dec_seq, dtype=F32), dec_batch)
    log_gamma = jnp.log1p(-jnp.exp2(-5.0 - jnp.arange(B_HEADS, dtype=F32)))

    tm_p = _row_tile(seq, ROW_TILE)
    tm_b = _row_tile(seq, ROW_TILE_NORMALIZED)
    tm_s = dec_batch * dec_seq
    tn = PROJ_COL_TILE

    a_widths = [d, d, d, d]
    a_dtypes = [BF16, BF16, F32, BF16]
    a_kernel = functools.partial(_inproj_a_kernel, q_scale=A_HEAD_DIM ** -0.5 * LOG2_E)
    k_by_head = [(1, A_HEAD_DIM)]
    qp, kp, vp, gp, khp = _inproj("inproj_a_prompt", a_kernel, xp, norm_a[0],
                                  _rope_tables_a(pos_p), w_in_a_b, a_widths, a_dtypes, tm_p, tn,
                                  by_head=k_by_head, cols_outer=True)
    qs, ks, vs, gs, khs = _inproj("inproj_a_sample", a_kernel, xs, norm_a[0],
                                  _rope_tables_a(pos_s), w_in_a_b, a_widths, a_dtypes, tm_s, tn,
                                  by_head=k_by_head)
    op = _attn_prompt(qp, kp, vp, gp, lam_params, subln_a[0], batch, seq, lambda_init,
                      tq=min(ATTN_TILE, seq))
    os_ = _attn_sample(qs, ks, vs, gs,
                       cache_k_a[0].reshape(dec_batch * past, d),
                       cache_v_a[0].reshape(dec_batch * past, d),
                       lam_params, subln_a[0], dec_batch, lambda_init, heads_per_step=4)
    xp1, xnp1 = _outproj("outproj_a_prompt", op, w_out_a_b, xp, tm_p, norm_b[0], final=False)
    xs1, xns1 = _outproj("outproj_a_sample", os_, w_out_a_b, xs, tm_s, norm_b[0], final=False)

    b_widths = [d, d, 2 * d, 2 * d]
    b_dtypes = [BF16, BF16, BF16, BF16]
    b_kernel = functools.partial(_inproj_b_kernel, k_scale=dk ** -0.5)
    qp2, kp2, vp2, gp2 = _inproj("inproj_b_prompt", b_kernel, xnp1, None,
                                 _xpos_tables_b(pos_p, dk), w_in_b_b, b_widths, b_dtypes, tm_b, tn)
    qs2, ks2, vs2, gs2 = _inproj("inproj_b_sample", b_kernel, xns1, None,
                                 _xpos_tables_b(pos_s, dk), w_in_b_b, b_widths, b_dtypes, tm_s, tn)
    rp, state_p = _retention("retention_prompt", log_gamma, qp2, kp2, vp2, gp2, batch,
                             rows=min(RETENTION_ROWS, seq), chunk=CHUNK)
    rs, state_s = _retention("retention_sample", log_gamma, qs2, ks2, vs2, gs2, dec_batch,
                             rows=dec_seq, chunk=dec_seq, init_state=state_ret[0])
    (yp,) = _outproj("outproj_b_prompt", rp, w_out_b_b, xp1, tm_p, norm_final, final=True)
    (ys,) = _outproj("outproj_b_sample", rs, w_out_b_b, xs1, tm_s, norm_final, final=True)

    return (yp.reshape(batch, seq, d),
            ys.reshape(dec_batch, dec_seq, d),
            khp.reshape(1, batch, seq, 2 * a_heads, A_HEAD_DIM),
            vp.reshape(1, batch, seq, a_heads, 2 * A_HEAD_DIM),
            state_p[None],
            khs.reshape(1, dec_batch, dec_seq, 2 * a_heads, A_HEAD_DIM),
            vs.reshape(1, dec_batch, dec_seq, a_heads, 2 * A_HEAD_DIM),
            state_s[None])
```

```python
import functools
import math

import jax
import jax.numpy as jnp
from jax import lax
from jax.experimental import pallas as pl
from jax.experimental.pallas import tpu as pltpu

F32 = jnp.float32
BF16 = jnp.bfloat16

CHUNK = 64
A_HEAD_DIM = 64
A_ROT_DIM = A_HEAD_DIM // 4
A_ROT_HALF = A_ROT_DIM // 2
ROPE_THETA = 500000.0
B_HEADS = 8
XPOS_BASE = 10000.0
NORM_EPS = 1e-6
SUBLN_EPS = 1e-5

LANES = 128
SUBLANES = 8
MXU_DIM = 256
SUM_ROWS = 16
LOG2_E = math.log2(math.e)
VMEM_LIMIT_BYTES = 56 * 1024 * 1024

ROW_TILE = 512
ROW_TILE_NORMALIZED = 1024
PROJ_COL_TILE = 512
ATTN_TILE = 256
RETENTION_ROWS = 256


def _compiler_params(n_axes):
    return pltpu.CompilerParams(
        dimension_semantics=("arbitrary",) * n_axes,
        vmem_limit_bytes=VMEM_LIMIT_BYTES)


def _rms_scale(x, eps):
    return lax.rsqrt(jnp.mean(x * x, axis=-1, keepdims=True) + eps)


def _silu(g):
    h = 0.5 * g
    return h + h * jnp.tanh(h)


def _dot(a, b):
    return jnp.dot(a, b, preferred_element_type=F32)


def _dot_nt(a, b):
    return lax.dot_general(a, b, (((1,), (1,)), ((), ())), preferred_element_type=F32)


def _dot_tn(a, b):
    return lax.dot_general(a, b, (((0,), (0,)), ((), ())), preferred_element_type=F32)


def _normalized_rows(x_ref, nw_ref, maybe_xn_ref):
    def normalize():
        x = x_ref[...]
        return (x * _rms_scale(x, NORM_EPS) * nw_ref[...]).astype(BF16)

    if not maybe_xn_ref:
        return normalize()
    (xn_ref,) = maybe_xn_ref

    @pl.when(pl.program_id(1) == 0)
    def _():
        xn_ref[...] = normalize()

    return xn_ref[...]


def _inproj_a_kernel(x_ref, nw_ref, cos_ref, sa_ref, sb_ref, wq_ref, wk_ref, wv_ref, wg_ref,
                     q_ref, k_ref, v_ref, g_ref, kh_ref, *maybe_xn_ref, q_scale):
    xn = _normalized_rows(x_ref, nw_ref, maybe_xn_ref)
    cos, sa, sb = cos_ref[...], sa_ref[...], sb_ref[...]

    def rope_store(z, out_ref, scale, heads_ref=None):
        rotated = []
        for c in range(z.shape[1] // LANES):
            zc = z[:, c * LANES:(c + 1) * LANES]
            r = (zc * cos + pltpu.roll(zc, A_ROT_HALF, 1) * sa
                 + pltpu.roll(zc, LANES - A_ROT_HALF, 1) * sb)
            if scale != 1.0:
                r = r * scale
            out_ref[:, c * LANES:(c + 1) * LANES] = r.astype(out_ref.dtype)
            rotated.append(r)
        if heads_ref is not None:
            heads_ref[...] = pltpu.einshape("m(ud)->mud", jnp.concatenate(rotated, axis=1),
                                            u=SUBLANES)

    rope_store(_dot(xn, wq_ref[...]), q_ref, q_scale)
    rope_store(_dot(xn, wk_ref[...]), k_ref, 1.0, kh_ref)
    v_ref[...] = _dot(xn, wv_ref[...])
    g_ref[...] = _dot(xn, wg_ref[...]).astype(g_ref.dtype)


def _inproj_b_kernel(xn_ref, cos_ref, sin_ref, wq_ref, wk_ref, wv_ref, wg_ref,
                     q_ref, k_ref, v_ref, g_ref, *, k_scale):
    xn = xn_ref[...]
    cos, sin = cos_ref[...], sin_ref[...]

    def rope_store(z, out_ref, scale):
        for hh in range(z.shape[1] // (2 * LANES)):
            c0 = hh * 2 * LANES
            lo = z[:, c0:c0 + LANES]
            hi = z[:, c0 + LANES:c0 + 2 * LANES]
            r_lo = lo * cos - hi * sin
            r_hi = lo * sin + hi * cos
            if scale != 1.0:
                r_lo, r_hi = r_lo * scale, r_hi * scale
            out_ref[:, c0:c0 + LANES] = r_lo.astype(out_ref.dtype)
            out_ref[:, c0 + LANES:c0 + 2 * LANES] = r_hi.astype(out_ref.dtype)

    rope_store(_dot(xn, wq_ref[...]), q_ref, 1.0)
    rope_store(_dot(xn, wk_ref[...]), k_ref, k_scale)
    v_ref[...] = _dot(xn, wv_ref[...]).astype(v_ref.dtype)
    g_ref[...] = _dot(xn, wg_ref[...]).astype(g_ref.dtype)


def _inproj(name, kernel_fn, x, norm_w, tables, w_bf16, widths, out_dtypes, tm, tn_unit,
            by_head=(), cols_outer=False):
    t, d = x.shape
    nj = min(widths) // tn_unit
    tns = [w // nj for w in widths]
    offs = [sum(widths[:s]) // tns[s] for s in range(len(widths))]
    n_pos_blocks = tables[0].shape[0] // tm

    def spec(block, index_map):
        if cols_outer:
            return pl.BlockSpec(block, lambda j, i: index_map(i, j))
        return pl.BlockSpec(block, index_map)

    in_specs = [spec((tm, d), lambda i, j: (i, 0))]
    args = [x]
    scratch_shapes = []
    if norm_w is not None:
        in_specs.append(spec((1, d), lambda i, j: (0, 0)))
        args.append(norm_w.reshape(1, d))
        if not cols_outer:
            scratch_shapes.append(pltpu.VMEM((tm, d), BF16))
    in_specs += [spec((tm, LANES), lambda i, j: (i % n_pos_blocks, 0)) for _ in tables]
    for s in range(len(widths)):
        in_specs.append(spec((d, tns[s]), functools.partial(
            lambda i, j, off: (0, off + j), off=offs[s])))
    out_specs = [spec((tm, tns[s]), lambda i, j: (i, j)) for s in range(len(widths))]
    out_shape = [jax.ShapeDtypeStruct((t, widths[s]), out_dtypes[s]) for s in range(len(widths))]
    for s, head_width in by_head:
        assert tns[s] == SUBLANES * head_width
        out_specs.append(spec((tm, SUBLANES, head_width), lambda i, j: (i, j, 0)))
        out_shape.append(jax.ShapeDtypeStruct((t, widths[s] // head_width, head_width), F32))
    return pl.pallas_call(
        kernel_fn,
        grid=(nj, t // tm) if cols_outer else (t // tm, nj),
        in_specs=in_specs,
        out_specs=out_specs,
        out_shape=out_shape,
        scratch_shapes=scratch_shapes,
        compiler_params=_compiler_params(2),
        name=name,
    )(*args, *tables, *([w_bf16] * len(widths)))


def _lambda_value(lam_ref, lambda_init):
    lp = lam_ref[...]
    s1 = jnp.sum(lp[0:1] * lp[1:2], axis=-1, keepdims=True)
    s2 = jnp.sum(lp[2:3] * lp[3:4], axis=-1, keepdims=True)
    return jnp.exp(s1) - jnp.exp(s2) + lambda_init


def _split_subheads(q):
    lane = lax.broadcasted_iota(jnp.int32, q.shape, 1)
    qf = q.astype(F32)
    return jnp.where(lane < A_HEAD_DIM, qf, 0.0), jnp.where(lane >= A_HEAD_DIM, qf, 0.0)


def _diff_finish(o1, o2, lam, sg, g, lambda_init):
    o = o1 - lam * o2
    on = o * _rms_scale(o, SUBLN_EPS) * sg
    on = on * (1.0 - lambda_init)
    return on * _silu(g)


def _attn_prompt_kernel(q_ref, k_ref, v_ref, g_ref, lam_ref, sg_ref, o_ref,
                        vt_ref, qt_ref, m_ref, acc_ref, s_ref, p_ref, a_ref,
                        *, tq, lambda_init):
    seq = q_ref.shape[0]
    n_tiles = seq // tq
    lam = _lambda_value(lam_ref, lambda_init)
    sg = sg_ref[...]
    ones_row = lax.broadcasted_iota(jnp.int32, (SUM_ROWS, tq), 0) == 0
    key = lax.broadcasted_iota(jnp.int32, (tq, tq), 0)
    qry = lax.broadcasted_iota(jnp.int32, (tq, tq), 1)
    diag_mask = (key // CHUNK) <= (qry // CHUNK)

    def prepare_tile(t):
        rows = slice(t * tq, (t + 1) * tq)
        vt_ref[t, 0:LANES] = v_ref[rows, :].T.astype(BF16)
        vt_ref[t, LANES:LANES + SUM_ROWS] = jnp.where(ones_row, 1.0, 0.0).astype(BF16)
        qa, qb = _split_subheads(q_ref[rows, :])
        qt_ref[t, 0] = qa.T.astype(BF16)
        qt_ref[t, 1] = qb.T.astype(BF16)
        m_ref[t] = jnp.full(m_ref.shape[1:], -jnp.inf, F32)
        acc_ref[t] = jnp.zeros(acc_ref.shape[1:], F32)

    def finish_tile(qi):
        rows = slice(qi * tq, (qi + 1) * tq)
        o1, o2 = [(acc_ref[qi, i, 0:LANES] * (1.0 / acc_ref[qi, i, LANES:LANES + 1])).T
                  for i in range(2)]
        out = _diff_finish(o1, o2, lam, sg, g_ref[rows, :].astype(F32), lambda_init)
        o_ref[rows, :] = out.astype(o_ref.dtype)

    items = [(qi, kt) for qi in range(n_tiles) for kt in range(qi + 1)]

    def score_stage(j):
        qi, kt = items[j]
        if kt == 0:
            prepare_tile(qi)
        for i in range(2):
            s_ref[j % 2, i] = _dot(k_ref[kt * tq:(kt + 1) * tq, :], qt_ref[qi, i])

    def softmax_stage(j):
        qi, kt = items[j]
        for i in range(2):
            s = s_ref[j % 2, i]
            if kt == qi:
                s = jnp.where(diag_mask, s, -jnp.inf)
            m_old = m_ref[qi, i]
            m_new = jnp.maximum(m_old, jnp.max(s, axis=0, keepdims=True))
            m_ref[qi, i] = m_new
            a_ref[j % 2, i] = jnp.exp2(m_old - m_new)
            p_ref[j % 2, i] = jnp.exp2(s - m_new).astype(BF16)

    def value_stage(j):
        qi, kt = items[j]
        for i in range(2):
            acc_ref[qi, i] = (a_ref[j % 2, i] * acc_ref[qi, i]
                              + _dot(vt_ref[kt], p_ref[j % 2, i]))
        if kt == qi:
            finish_tile(qi)

    for step in range(len(items) + 2):
        for stage, lag in ((value_stage, 2), (softmax_stage, 1), (score_stage, 0)):
            if 0 <= step - lag < len(items):
                stage(step - lag)


def _attn_prompt(q, k, v, g, lam_params, subln, batch, seq, lambda_init, tq):
    t, width = q.shape
    heads = width // LANES
    n_tiles = seq // tq
    blk = lambda: pl.BlockSpec((seq, LANES), lambda b, h: (b, h))
    return pl.pallas_call(
        functools.partial(_attn_prompt_kernel, tq=tq, lambda_init=lambda_init),
        grid=(batch, heads),
        in_specs=[blk(), blk(), blk(), blk(),
                  pl.BlockSpec(lam_params.shape, lambda b, h: (0, 0)),
                  pl.BlockSpec((1, LANES), lambda b, h: (0, 0))],
        out_specs=blk(),
        out_shape=jax.ShapeDtypeStruct((t, width), BF16),
        scratch_shapes=[pltpu.VMEM((n_tiles, LANES + SUM_ROWS, tq), BF16),
                        pltpu.VMEM((n_tiles, 2, LANES, tq), BF16),
                        pltpu.VMEM((n_tiles, 2, 1, tq), F32),
                        pltpu.VMEM((n_tiles, 2, LANES + SUM_ROWS, tq), F32),
                        pltpu.VMEM((2, 2, tq, tq), F32),
                        pltpu.VMEM((2, 2, tq, tq), BF16),
                        pltpu.VMEM((2, 2, 1, tq), F32)],
        compiler_params=_compiler_params(2),
        name="attn_prompt",
    )(q, k, v, g, lam_params, subln.reshape(1, LANES))


def _attn_sample_kernel(q_ref, kn_ref, vn_ref, ck_ref, cv_ref, g_ref, lam_ref, sg_ref, o_ref,
                        *, heads_per_step, lambda_init):
    lam = _lambda_value(lam_ref, lambda_init)
    sg = sg_ref[...]
    for hh in range(heads_per_step):
        cs = slice(hh * LANES, (hh + 1) * LANES)
        qs = jnp.concatenate(_split_subheads(q_ref[:, cs]), axis=0).astype(BF16)
        kc = ck_ref[:, cs].astype(BF16)
        vc = cv_ref[:, cs].astype(BF16)
        kn = kn_ref[:, cs].astype(BF16)
        vn = vn_ref[:, cs].astype(BF16)
        sc = _dot_nt(qs, kc)
        sn = _dot_nt(qs, kn)
        m = jnp.maximum(jnp.max(sc, axis=-1, keepdims=True),
                        jnp.max(sn, axis=-1, keepdims=True))
        pc = jnp.exp2(sc - m)
        pn = jnp.exp2(sn - m)
        l = jnp.sum(pc, axis=-1, keepdims=True) + jnp.sum(pn, axis=-1, keepdims=True)
        acc = _dot(pc.astype(BF16), vc) + _dot(pn.astype(BF16), vn)
        o12 = acc * (1.0 / l)
        frames = q_ref.shape[0]
        out = _diff_finish(o12[:frames], o12[frames:], lam, sg, g_ref[:, cs].astype(F32),
                           lambda_init)
        o_ref[:, cs] = out.astype(o_ref.dtype)


def _attn_sample(q, k, v, g, cache_k, cache_v, lam_params, subln, batch, lambda_init,
                 heads_per_step):
    t, width = q.shape
    frames = t // batch
    past = cache_k.shape[0] // batch
    cols = heads_per_step * LANES
    new = lambda: pl.BlockSpec((frames, cols), lambda b, h: (b, h))
    old = lambda: pl.BlockSpec((past, cols), lambda b, h: (b, h))
    return pl.pallas_call(
        functools.partial(_attn_sample_kernel, heads_per_step=heads_per_step,
                          lambda_init=lambda_init),
        grid=(batch, width // cols),
        in_specs=[new(), new(), new(), old(), old(), new(),
                  pl.BlockSpec(lam_params.shape, lambda b, h: (0, 0)),
                  pl.BlockSpec((1, LANES), lambda b, h: (0, 0))],
        out_specs=new(),
        out_shape=jax.ShapeDtypeStruct((t, width), BF16),
        compiler_params=_compiler_params(2),
        name="attn_sample",
    )(q, k, v, cache_k, cache_v, g, lam_params, subln.reshape(1, LANES))


def _outproj_next_kernel(a_ref, w_ref, x_ref, nw_ref, y_ref, yn_ref):
    xo = x_ref[...] + _dot(a_ref[...], w_ref[...])
    y_ref[...] = xo
    yn_ref[...] = (xo * _rms_scale(xo, NORM_EPS) * nw_ref[...]).astype(yn_ref.dtype)


def _outproj_final_kernel(a_ref, w_ref, x_ref, nf_ref, y_ref):
    xo = x_ref[...] + _dot(a_ref[...], w_ref[...])
    y_ref[...] = xo * _rms_scale(xo, NORM_EPS) * nf_ref[...]


def _outproj(name, a, w_bf16, x, tm, norm_w, final):
    t, kdim = a.shape
    d = x.shape[1]
    row_block = lambda width: pl.BlockSpec((tm, width), lambda i: (i, 0))
    out_specs = [row_block(d)]
    out_shape = [jax.ShapeDtypeStruct((t, d), F32)]
    if not final:
        out_specs.append(row_block(d))
        out_shape.append(jax.ShapeDtypeStruct((t, d), BF16))
    return pl.pallas_call(
        _outproj_final_kernel if final else _outproj_next_kernel,
        grid=(t // tm,),
        in_specs=[row_block(kdim),
                  pl.BlockSpec((kdim, d), lambda i: (0, 0), pipeline_mode=pl.Buffered(1)),
                  row_block(d),
                  pl.BlockSpec((1, d), lambda i: (0, 0))],
        out_specs=out_specs,
        out_shape=out_shape,
        compiler_params=_compiler_params(1),
        name=name,
    )(a, w_bf16, x, norm_w.reshape(1, d))


def _retention_kernel(*refs, rows, chunk, has_init):
    if has_init:
        lg_ref, q_ref, k_ref, v_ref, g_ref, s0_ref, o_ref, s_ref, y_ref = refs
    else:
        lg_ref, q_ref, k_ref, v_ref, g_ref, o_ref, s_ref, y_ref = refs
    n_blocks = q_ref.shape[0] // rows
    lg = lg_ref[pl.program_id(1)]
    row = lax.broadcasted_iota(jnp.int32, (rows, rows), 0)
    col = lax.broadcasted_iota(jnp.int32, (rows, rows), 1)
    dist = jnp.abs(row - col).astype(F32)
    decay = jnp.where((col // chunk) <= (row // chunk), jnp.exp(lg * dist), 0.0)
    idx = lax.broadcasted_iota(jnp.int32, (rows, 1), 0).astype(F32)
    q_decay = jnp.exp(lg * (idx + 1.0))
    k_decay = jnp.exp(lg * (rows - 1.0 - idx))
    block_decay = jnp.exp(jnp.full((1, 1), rows, F32) * lg)

    if has_init:
        s_ref[...] = s0_ref[...]
    else:
        s_ref[...] = jnp.zeros(s_ref.shape, F32)

    def mix_block(r):
        sl = slice(r * rows, (r + 1) * rows)
        q, k, v = q_ref[sl, :], k_ref[sl, :], v_ref[sl, :]
        state = s_ref[...]
        scores = _dot_nt(q, k) * decay
        y_ref[r % 2] = _dot(scores.astype(BF16), v) + q_decay * _dot(q, state.astype(BF16))
        kd = (k.astype(F32) * k_decay).astype(BF16)
        s_ref[...] = block_decay * state + _dot_tn(kd, v)

    def gate_block(r):
        sl = slice(r * rows, (r + 1) * rows)
        y = y_ref[r % 2]
        yn = y * _rms_scale(y, NORM_EPS)
        o_ref[sl, :] = (yn * _silu(g_ref[sl, :].astype(F32))).astype(o_ref.dtype)

    for r in range(n_blocks + 1):
        if r < n_blocks:
            mix_block(r)
        if r >= 1:
            gate_block(r - 1)


def _retention(name, log_gamma, q, k, v, g, batch, rows, chunk, init_state=None):
    t = q.shape[0]
    seq = t // batch
    dk = q.shape[1] // B_HEADS
    dv = v.shape[1] // B_HEADS
    in_specs = [pl.BlockSpec(memory_space=pltpu.SMEM),
                pl.BlockSpec((seq, dk), lambda b, h: (b, h)),
                pl.BlockSpec((seq, dk), lambda b, h: (b, h)),
                pl.BlockSpec((seq, dv), lambda b, h: (b, h)),
                pl.BlockSpec((seq, dv), lambda b, h: (b, h))]
    args = [log_gamma, q, k, v, g]
    state_spec = lambda: pl.BlockSpec((None, None, dk, dv), lambda b, h: (b, h, 0, 0))
    if init_state is not None:
        in_specs.append(state_spec())
        args.append(init_state)
    return pl.pallas_call(
        functools.partial(_retention_kernel, rows=rows, chunk=chunk,
                          has_init=init_state is not None),
        grid=(batch, B_HEADS),
        in_specs=in_specs,
        out_specs=[pl.BlockSpec((seq, dv), lambda b, h: (b, h)), state_spec()],
        out_shape=[jax.ShapeDtypeStruct((t, v.shape[1]), BF16),
                   jax.ShapeDtypeStruct((batch, B_HEADS, dk, dv), F32)],
        scratch_shapes=[pltpu.VMEM((2, rows, dv), F32)],
        compiler_params=_compiler_params(2),
        name=name,
    )(*args)


def _rope_tables_a(pos):
    inv = ROPE_THETA ** (-jnp.arange(0, A_ROT_DIM, 2, dtype=F32) / A_ROT_DIM)
    ang = pos[:, None] * inv[None, :]
    cos, sin = jnp.cos(ang), jnp.sin(ang)
    d = jnp.arange(LANES) % A_HEAD_DIM
    first = d < A_ROT_HALF
    second = (d >= A_ROT_HALF) & (d < A_ROT_DIM)
    idx = d % A_ROT_HALF
    cos_t = jnp.where((first | second)[None, :], cos[:, idx], 1.0)
    from_left = jnp.where(second[None, :], sin[:, idx], 0.0)
    from_right = jnp.where(first[None, :], -sin[:, idx], 0.0)
    return cos_t, from_left, from_right


def _xpos_tables_b(pos, dk):
    inv = 1.0 / (XPOS_BASE ** jnp.linspace(0.0, 1.0, dk // 2, dtype=F32))
    ang = pos[:, None] * inv[None, :]
    return jnp.cos(ang), jnp.sin(ang)


def _row_tile(t, cap):
    return cap if t % cap == 0 else t


def kernel(x_prompt, x_sample, cache_k_a, cache_v_a, state_ret, norm_a, w_in_a, lambda_q1,
           lambda_k1, lambda_q2, lambda_k2, subln_a, w_out_a, norm_b, w_in_b, w_out_b,
           norm_final):
    batch, seq, d = x_prompt.shape
    dec_batch, dec_seq, _ = x_sample.shape
    past = cache_k_a.shape[2]
    a_heads = d // (2 * A_HEAD_DIM)
    dk = d // B_HEADS
    dv = 2 * d // B_HEADS
    lambda_init = 0.8 - 0.6 * math.exp(-0.3 * 0)

    xp = x_prompt.reshape(batch * seq, d)
    xs = x_sample.reshape(dec_batch * dec_seq, d)

    w_in_a_b = w_in_a[0].astype(BF16)
    w_out_a_b = w_out_a[0].astype(BF16)
    w_in_b_b = w_in_b[0].astype(BF16)
    w_out_b_b = w_out_b[0].astype(BF16)
    lam_params = jnp.stack([lambda_q1[0], lambda_k1[0], lambda_q2[0], lambda_k2[0]])

    pos_p = jnp.arange(seq, dtype=F32)
    pos_s = jnp.tile(past + jnp.arange(dec_seq, dtype=F32), dec_batch)
    log_gamma = jnp.log1p(-jnp.exp2(-5.0 - jnp.arange(B_HEADS, dtype=F32)))

    tm_p = _row_tile(seq, ROW_TILE)
    tm_b = _row_tile(seq, ROW_TILE_NORMALIZED)
    tm_s = dec_batch * dec_seq
    tn = PROJ_COL_TILE

    a_widths = [d, d, d, d]
    a_dtypes = [BF16, BF16, F32, BF16]
    a_kernel = functools.partial(_inproj_a_kernel, q_scale=A_HEAD_DIM ** -0.5 * LOG2_E)
    k_by_head = [(1, A_HEAD_DIM)]
    qp, kp, vp, gp, khp = _inproj("inproj_a_prompt", a_kernel, xp, norm_a[0],
                                  _rope_tables_a(pos_p), w_in_a_b, a_widths, a_dtypes, tm_p, tn,
                                  by_head=k_by_head, cols_outer=True)
    qs, ks, vs, gs, khs = _inproj("inproj_a_sample", a_kernel, xs, norm_a[0],
                                  _rope_tables_a(pos_s), w_in_a_b, a_widths, a_dtypes, tm_s, tn,
                                  by_head=k_by_head)
    op = _attn_prompt(qp, kp, vp, gp, lam_params, subln_a[0], batch, seq, lambda_init,
                      tq=min(ATTN_TILE, seq))
    os_ = _attn_sample(qs, ks, vs, gs,
                       cache_k_a[0].reshape(dec_batch * past, d),
                       cache_v_a[0].reshape(dec_batch * past, d),
                       lam_params, subln_a[0], dec_batch, lambda_init, heads_per_step=4)
    xp1, xnp1 = _outproj("outproj_a_prompt", op, w_out_a_b, xp, tm_p, norm_b[0], final=False)
    xs1, xns1 = _outproj("outproj_a_sample", os_, w_out_a_b, xs, tm_s, norm_b[0], final=False)

    b_widths = [d, d, 2 * d, 2 * d]
    b_dtypes = [BF16, BF16, BF16, BF16]
    b_kernel = functools.partial(_inproj_b_kernel, k_scale=dk ** -0.5)
    qp2, kp2, vp2, gp2 = _inproj("inproj_b_prompt", b_kernel, xnp1, None,
                                 _xpos_tables_b(pos_p, dk), w_in_b_b, b_widths, b_dtypes, tm_b, tn,
                                 cols_outer=True)
    qs2, ks2, vs2, gs2 = _inproj("inproj_b_sample", b_kernel, xns1, None,
                                 _xpos_tables_b(pos_s, dk), w_in_b_b, b_widths, b_dtypes, tm_s, tn)
    rp, state_p = _retention("retention_prompt", log_gamma, qp2, kp2, vp2, gp2, batch,
                             rows=min(RETENTION_ROWS, seq), chunk=CHUNK)
    rs, state_s = _retention("retention_sample", log_gamma, qs2, ks2, vs2, gs2, dec_batch,
                             rows=dec_seq, chunk=dec_seq, init_state=state_ret[0])
    (yp,) = _outproj("outproj_b_prompt", rp, w_out_b_b, xp1, tm_p, norm_final, final=True)
    (ys,) = _outproj("outproj_b_sample", rs, w_out_b_b, xs1, tm_s, norm_final, final=True)

    return (yp.reshape(batch, seq, d),
            ys.reshape(dec_batch, dec_seq, d),
            khp.reshape(1, batch, seq, 2 * a_heads, A_HEAD_DIM),
            vp.reshape(1, batch, seq, a_heads, 2 * A_HEAD_DIM),
            state_p[None],
            khs.reshape(1, dec_batch, dec_seq, 2 * a_heads, A_HEAD_DIM),
            vs.reshape(1, dec_batch, dec_seq, a_heads, 2 * A_HEAD_DIM),
            state_s[None])
```
